```python
import math
import jax, jax.numpy as jnp
from jax import lax
import numpy as np

D_MODEL = 1024
BATCH = 8
SEQ = 8192
DEPTH = 1

EPS = 1e-5
ROPE_THETA = 10000.0
GLA_HEADS = 4
GLA_DK = 64
GLA_DV = 128
GLA_GATE_RANK = 16
GLA_GATE_TEMP = 16.0
GLA_CHUNK = 64
GLA_QK = GLA_HEADS * GLA_DK
GLA_V = GLA_HEADS * GLA_DV
DIL_PAIRS = ((128, 1), (512, 4), (2048, 16))
DIL_GROUPS = 3
DIL_HEADS = 4
DIL_HEAD_DIM = 128
DIL_BLOCK = 128
DIL_W = DIL_GROUPS * DIL_HEADS * DIL_HEAD_DIM
DIL_OUT = DIL_HEADS * DIL_HEAD_DIM
N_BRANCHES = 2
MEM_LEN = 256
XATTN_HEADS = 4
XATTN_HEAD_DIM = 128
N_EXPERTS = 32
TOP_K = 4
D_FF = D_MODEL
SWIGLU_ALPHA = 1.702
SWIGLU_LIMIT = 7.0
MOE_BLOCK = 256
IN_SIZES = (GLA_QK, GLA_QK, GLA_V, GLA_V, GLA_GATE_RANK, DIL_W, DIL_W, DIL_W)
D_IN = 2 * GLA_QK + 2 * GLA_V + GLA_GATE_RANK + 3 * DIL_W

kernel_name = "hybrid_gla_dilated_moe_block"


def rms_norm(x, g):
    xf = x.astype(jnp.float32)
    y = xf * lax.rsqrt(jnp.mean(xf * xf, axis=-1, keepdims=True) + EPS)
    return (y * g.astype(jnp.float32)).astype(x.dtype)


def apply_rotary(t, positions):
    half = t.shape[-1] // 2
    inv_freq = ROPE_THETA ** (-jnp.arange(half, dtype=jnp.float32) / half)
    ang = positions.astype(jnp.float32)[..., None] * inv_freq
    ang = ang.reshape(ang.shape[:2] + (1,) * (t.ndim - 3) + (half,))
    cos, sin = jnp.cos(ang), jnp.sin(ang)
    tf = t.astype(jnp.float32)
    t1, t2 = tf[..., :half], tf[..., half:]
    return jnp.concatenate([t1 * cos - t2 * sin, t2 * cos + t1 * sin], axis=-1).astype(t.dtype)


def gla_chunked(q, k, v, log_a):
    B_, S_, H, dk = q.shape
    dv = v.shape[-1]
    n = S_ // GLA_CHUNK

    def chunks(t):
        return t.astype(jnp.float32).reshape(B_, n, GLA_CHUNK, H, t.shape[-1]).transpose(1, 0, 3, 2, 4)

    causal = jnp.tril(jnp.ones((GLA_CHUNK, GLA_CHUNK), dtype=bool))

    def step(state, xs):
        qc, kc, vc, lac = xs
        bcum = jnp.cumsum(lac, axis=2)
        blast = bcum[:, :, -1]
        inter = jnp.einsum('bhck,bhkv->bhcv', qc * jnp.exp(bcum), state)
        diff = bcum[:, :, :, None, :] - bcum[:, :, None, :, :]
        decay = jnp.exp(jnp.where(causal[None, None, :, :, None], diff, -jnp.inf))
        scores = jnp.einsum('bhtsk,bhsk->bhts', qc[:, :, :, None, :] * decay, kc)
        intra = jnp.einsum('bhts,bhsv->bhtv', scores, vc)
        k_dec = kc * jnp.exp(blast[:, :, None, :] - bcum)
        new_state = jnp.exp(blast)[..., None] * state + jnp.einsum('bhsk,bhsv->bhkv', k_dec, vc)
        return new_state, inter + intra

    state0 = jnp.zeros((B_, H, dk, dv), jnp.float32)
    _, o = lax.scan(step, state0, (chunks(q), chunks(k), chunks(v), chunks(log_a)))
    return o.transpose(1, 0, 3, 2, 4).reshape(B_, S_, H, dv)


def dilated_window_attention(q, k, v, window, dilation):
    B_, S_, H, dh = q.shape
    span = window // dilation
    L = S_ // dilation
    nb = -(-L // DIL_BLOCK)
    Lp = nb * DIL_BLOCK

    def to_blocks(t):
        t = t.reshape(B_, L, dilation, H, dh).transpose(0, 2, 1, 3, 4)
        t = jnp.pad(t, ((0, 0), (0, 0), (0, Lp - L), (0, 0), (0, 0)))
        return t.reshape(B_, dilation, nb, DIL_BLOCK, H, dh)

    def with_prev(t):
        prev = jnp.pad(t[:, :, :-1], ((0, 0), (0, 0), (1, 0), (0, 0), (0, 0), (0, 0)))
        return jnp.concatenate([prev, t], axis=3)

    qb = to_blocks(q)
    kc = with_prev(to_blocks(k))
    vc = with_prev(to_blocks(v))
    s = jnp.einsum('bpnqhd,bpnkhd->bpnhqk', qb, kc).astype(jnp.float32)
    qi = jnp.arange(DIL_BLOCK)[:, None] + DIL_BLOCK
    kj = jnp.arange(2 * DIL_BLOCK)[None, :]
    dist = qi - kj
    band = (dist >= 0) & (dist <= span)
    has_prev = (jnp.arange(nb)[:, None, None] > 0) | (kj >= DIL_BLOCK)[None]
    mask = band[None] & has_prev
    s = jnp.where(mask[None, None, :, None], s, -jnp.inf)
    lse = jax.nn.logsumexp(s, axis=-1)
    p = jnp.exp(s - lse[..., None])
    o = jnp.einsum('bpnhqk,bpnkhd->bpnqhd', p.astype(v.dtype), vc)
    o = o.reshape(B_, dilation, Lp, H, dh)[:, :, :L].transpose(0, 2, 1, 3, 4).reshape(B_, S_, H, dh)
    lse = lse.transpose(0, 1, 2, 4, 3).reshape(B_, dilation, Lp, H)[:, :, :L]
    lse = lse.transpose(0, 2, 1, 3).reshape(B_, S_, H)
    return o, lse


def hybrid_mixer(a, positions, w_in, w_gla_a2, b_gla_a2, gla_norm_g, w_branch_gla, w_branch_dil,
                 w_branch_gate, b_branch_gate, w_mix_out):
    B_, S_, _ = a.shape
    proj = a @ w_in
    offs = [int(o) for o in np.cumsum(IN_SIZES)[:-1]]
    q_g, k_g, v_g, r_g, a_lr, q_d, k_d, v_d = jnp.split(proj, offs, axis=-1)

    q_g = q_g.reshape(B_, S_, GLA_HEADS, GLA_DK) * (GLA_DK ** -0.5)
    k_g = k_g.reshape(B_, S_, GLA_HEADS, GLA_DK)
    v_g = v_g.reshape(B_, S_, GLA_HEADS, GLA_DV)
    gate_logit = (a_lr @ w_gla_a2 + b_gla_a2).astype(jnp.float32)
    log_a = (jax.nn.log_sigmoid(gate_logit) / GLA_GATE_TEMP).reshape(B_, S_, GLA_HEADS, GLA_DK)
    o_g = gla_chunked(q_g, k_g, v_g, log_a)
    o_g = o_g * lax.rsqrt(jnp.mean(o_g * o_g, axis=-1, keepdims=True) + EPS)
    o_g = o_g.reshape(B_, S_, GLA_V) * gla_norm_g.astype(jnp.float32)
    o_g = (o_g * jax.nn.silu(r_g.astype(jnp.float32))).astype(a.dtype)

    shp = (B_, S_, DIL_GROUPS, DIL_HEADS, DIL_HEAD_DIM)
    q_d = apply_rotary(q_d.reshape(shp), positions) * (DIL_HEAD_DIM ** -0.5)
    k_d = apply_rotary(k_d.reshape(shp), positions)
    v_d = v_d.reshape(shp)
    outs, lses = [], []
    for g, (window, dilation) in enumerate(DIL_PAIRS):
        o_i, lse_i = dilated_window_attention(q_d[:, :, g], k_d[:, :, g], v_d[:, :, g], window, dilation)
        outs.append(o_i)
        lses.append(lse_i)
    o_all = jnp.stack(outs, axis=2)
    wts = jax.nn.softmax(jnp.stack(lses, axis=2), axis=2)
    o_d = jnp.sum(wts[..., None] * o_all.astype(jnp.float32), axis=2)
    o_d = o_d.reshape(B_, S_, DIL_OUT).astype(a.dtype)

    gates = jax.nn.sigmoid((a @ w_branch_gate + b_branch_gate).astype(jnp.float32))
    g_gla, g_dil = gates[..., :D_MODEL], gates[..., D_MODEL:]
    merged = g_gla * (o_g @ w_branch_gla).astype(jnp.float32) + g_dil * (o_d @ w_branch_dil).astype(jnp.float32)
    return merged.astype(a.dtype) @ w_mix_out


def memory_cross_attention(c, mem, mem_norm_g, w_xq, w_xkv, w_xo):
    B_, S_, _ = c.shape
    q = (c @ w_xq).reshape(B_, S_, XATTN_HEADS, XATTN_HEAD_DIM) * (XATTN_HEAD_DIM ** -0.5)
    kv = (rms_norm(mem, mem_norm_g) @ w_xkv).reshape(B_, MEM_LEN, 2, XATTN_HEADS, XATTN_HEAD_DIM)
    k, v = kv[:, :, 0], kv[:, :, 1]
    s = jnp.einsum('bshd,bmhd->bhsm', q, k).astype(jnp.float32)
    p = jax.nn.softmax(s, axis=-1)
    o = jnp.einsum('bhsm,bmhd->bshd', p.astype(v.dtype), v).reshape(B_, S_, XATTN_HEADS * XATTN_HEAD_DIM)
    return o @ w_xo


def clamped_swiglu(glu, lin):
    glu = jnp.minimum(glu, SWIGLU_LIMIT)
    lin = jnp.clip(lin, -SWIGLU_LIMIT, SWIGLU_LIMIT)
    return glu * jax.nn.sigmoid(SWIGLU_ALPHA * glu) * (lin + 1.0)


def moe_ffn(f, w_router, b_router, w_gate, b_gate, w_up, b_up, w_down, b_down):
    B_, S_, D = f.shape
    M = B_ * S_
    hf = f.reshape(M, D)
    logits = (hf @ w_router).astype(jnp.float32) + b_router.astype(jnp.float32)
    top_logit, top_e = lax.top_k(logits, TOP_K)
    gate = jax.nn.softmax(top_logit, axis=-1)
    P = M * TOP_K
    e_flat = top_e.reshape(P)
    tok_flat = jnp.arange(P, dtype=jnp.int32) // TOP_K
    g_flat = gate.reshape(P)
    order = jnp.argsort(e_flat)
    e_sorted = e_flat[order]
    counts = jnp.bincount(e_flat, length=N_EXPERTS)
    start = jnp.cumsum(counts) - counts
    padded = ((counts + MOE_BLOCK - 1) // MOE_BLOCK) * MOE_BLOCK
    pend = jnp.cumsum(padded)
    pstart = pend - padded
    dest = pstart[e_sorted] + (jnp.arange(P) - start[e_sorted])
    L = (-(-P // MOE_BLOCK) + N_EXPERTS) * MOE_BLOCK
    row_tok = jnp.full((L,), M, jnp.int32).at[dest].set(tok_flat[order])
    row_gate = jnp.zeros((L,), jnp.float32).at[dest].set(g_flat[order])
    n_blocks = L // MOE_BLOCK
    block_e = jnp.minimum(jnp.searchsorted(pend, jnp.arange(n_blocks) * MOE_BLOCK, side='right'),
                          N_EXPERTS - 1)
    h_pad = jnp.concatenate([hf, jnp.zeros((1, D), hf.dtype)], axis=0)
    xs = h_pad[row_tok].reshape(n_blocks, MOE_BLOCK, D)

    def expert_block(args):
        xb, e = args
        glu = xb @ w_gate[e] + b_gate[e]
        lin = xb @ w_up[e] + b_up[e]
        return clamped_swiglu(glu, lin) @ w_down[e] + b_down[e]

    ys = lax.map(expert_block, (xs, block_e)).reshape(L, D).astype(jnp.float32) * row_gate[:, None]
    out = jnp.zeros((M + 1, D), jnp.float32).at[row_tok].add(ys)[:M]
    return out.reshape(B_, S_, D).astype(f.dtype)


def setup_inputs(seed: int = 0) -> dict:
    key = jax.random.key(seed)
    ks = jax.random.split(key, 32)
    D = D_MODEL
    nrm = lambda k, shape, fan_in: jax.random.normal(k, shape, jnp.float32) * (fan_in ** -0.5)
    gain = lambda k, shape: 1.0 + 0.02 * jax.random.normal(k, shape, jnp.float32)
    bias = lambda k, shape, s=0.02: s * jax.random.normal(k, shape, jnp.float32)
    x = jax.random.normal(ks[0], (BATCH, SEQ, D), jnp.float32)
    mem = jax.random.normal(ks[1], (BATCH, MEM_LEN, D), jnp.float32)
    offset = jax.random.randint(ks[2], (BATCH, 1), 0, 1024, dtype=jnp.int32)
    positions = jnp.arange(SEQ, dtype=jnp.int32)[None, :] + offset
    return {
        "x": x,
        "mem": mem,
        "positions": positions,
        "norm1_g": gain(ks[3], (DEPTH, D)),
        "w_in": nrm(ks[4], (DEPTH, D, D_IN), D),
        "w_gla_a2": nrm(ks[5], (DEPTH, GLA_GATE_RANK, GLA_QK), GLA_GATE_RANK),
        "b_gla_a2": bias(ks[6], (DEPTH, GLA_QK), 0.1),
        "gla_norm_g": gain(ks[7], (DEPTH, GLA_V)),
        "w_branch_gla": nrm(ks[8], (DEPTH, GLA_V, D), GLA_V),
        "w_branch_dil": nrm(ks[9], (DEPTH, DIL_OUT, D), DIL_OUT),
        "w_branch_gate": nrm(ks[10], (DEPTH, D, N_BRANCHES * D), D),
        "b_branch_gate": bias(ks[11], (DEPTH, N_BRANCHES * D)),
        "w_mix_out": nrm(ks[12], (DEPTH, D, D), D),
        "norm2_g": gain(ks[13], (DEPTH, D)),
        "mem_norm_g": gain(ks[14], (DEPTH, D)),
        "w_xq": nrm(ks[15], (DEPTH, D, XATTN_HEADS * XATTN_HEAD_DIM), D),
        "w_xkv": nrm(ks[16], (DEPTH, D, 2 * XATTN_HEADS * XATTN_HEAD_DIM), D),
        "w_xo": nrm(ks[17], (DEPTH, XATTN_HEADS * XATTN_HEAD_DIM, D), XATTN_HEADS * XATTN_HEAD_DIM),
        "norm3_g": gain(ks[18], (DEPTH, D)),
        "w_router": nrm(ks[19], (DEPTH, D, N_EXPERTS), D),
        "b_router": bias(ks[20], (DEPTH, N_EXPERTS), 0.01),
        "w_gate": nrm(ks[21], (DEPTH, N_EXPERTS, D, D_FF), D),
        "b_gate": bias(ks[22], (DEPTH, N_EXPERTS, D_FF)),
        "w_up": nrm(ks[23], (DEPTH, N_EXPERTS, D, D_FF), D),
        "b_up": bias(ks[24], (DEPTH, N_EXPERTS, D_FF)),
        "w_down": nrm(ks[25], (DEPTH, N_EXPERTS, D_FF, D), D_FF),
        "b_down": bias(ks[26], (DEPTH, N_EXPERTS, D)),
        "final_norm_g": gain(ks[27], (D,)),
    }


def reference(x, mem, positions, norm1_g, w_in, w_gla_a2, b_gla_a2, gla_norm_g, w_branch_gla,
              w_branch_dil, w_branch_gate, b_branch_gate, w_mix_out, norm2_g, mem_norm_g, w_xq, w_xkv,
              w_xo, norm3_g, w_router, b_router, w_gate, b_gate, w_up, b_up, w_down, b_down,
              final_norm_g):
    h = x
    for l in range(DEPTH):
        a = rms_norm(h, norm1_g[l])
        h = h + hybrid_mixer(a, positions, w_in[l], w_gla_a2[l], b_gla_a2[l], gla_norm_g[l],
                             w_branch_gla[l], w_branch_dil[l], w_branch_gate[l], b_branch_gate[l],
                             w_mix_out[l])
        c = rms_norm(h, norm2_g[l])
        h = h + memory_cross_attention(c, mem, mem_norm_g[l], w_xq[l], w_xkv[l], w_xo[l])
        f = rms_norm(h, norm3_g[l])
        h = h + moe_ffn(f, w_router[l], b_router[l], w_gate[l], b_gate[l], w_up[l], b_up[l],
                        w_down[l], b_down[l])
    return rms_norm(h, final_norm_g)
```

```python
import functools

import numpy as np
import jax
import jax.numpy as jnp
from jax import lax
from jax.experimental import pallas as pl
from jax.experimental.pallas import tpu as pltpu

F32 = jnp.float32
BF16 = jnp.bfloat16
I32 = jnp.int32

EPS = 1e-5
ROPE_THETA = 10000.0
GLA_HEADS = 4
GLA_DK = 64
GLA_DV = 128
GLA_GATE_RANK = 16
GLA_GATE_TEMP = 16.0
GLA_CHUNK = 64
GLA_QK = GLA_HEADS * GLA_DK
GLA_V = GLA_HEADS * GLA_DV
DIL_PAIRS = ((128, 1), (512, 4), (2048, 16))
DIL_GROUPS = 3
DIL_HEADS = 4
DIL_HEAD_DIM = 128
DIL_BLOCK = 128
DIL_GW = DIL_HEADS * DIL_HEAD_DIM
DIL_W = DIL_GROUPS * DIL_GW
XATTN_HEADS = 4
XATTN_HEAD_DIM = 128
N_EXPERTS = 32
TOP_K = 4
SWIGLU_ALPHA = 1.702
SWIGLU_LIMIT = 7.0

LANES = 128
NEG_BIG = -1e30
VMEM_LIMIT = 56 * 1024 * 1024

TOKEN_TILE = 512
POST_TILE = 256
GLA_TILE = 512
MOE_ROWS = 256
COMBINE_TILE = 256


def _params(*sem):
    return pltpu.CompilerParams(dimension_semantics=sem, vmem_limit_bytes=VMEM_LIMIT)


def _resident(shape):
    nd = len(shape)
    return pl.BlockSpec(shape, lambda *_: (0,) * nd, pipeline_mode=pl.Buffered(1))


def _rms(x, g):
    return x * lax.rsqrt(jnp.mean(x * x, axis=-1, keepdims=True) + EPS) * g


def _dot(a, b):
    return jnp.dot(a, b, preferred_element_type=F32)


def _dot_nt(a, b):
    return lax.dot_general(a, b, (((1,), (1,)), ((), ())), preferred_element_type=F32)


def _gla_proj_kernel(x_ref, g_ref, w_ref, walr_ref, wa2_ref, ba2_ref,
                     q_ref, k_ref, v_ref, r_ref, la_ref):
    a = _rms(x_ref[...], g_ref[...]).astype(BF16)
    y = _dot(a, w_ref[...])
    q_ref[...] = (y[:, :GLA_QK] * (GLA_DK ** -0.5)).astype(BF16)
    k_ref[...] = y[:, GLA_QK:2 * GLA_QK].astype(BF16)
    v_ref[...] = y[:, 2 * GLA_QK:2 * GLA_QK + GLA_V].astype(BF16)
    r = y[:, 2 * GLA_QK + GLA_V:]
    r_ref[...] = (r * jax.nn.sigmoid(r)).astype(BF16)
    a_lr = _dot(a, walr_ref[...])
    logit = _dot(a_lr.astype(BF16), wa2_ref[...]) + ba2_ref[...]
    log_sig = jnp.minimum(logit, 0.0) - jnp.log1p(jnp.exp(-jnp.abs(logit)))
    la_ref[...] = log_sig * (1.0 / GLA_GATE_TEMP)


def _gla_proj(x2, g, w, walr, wa2, ba2):
    m, d = x2.shape
    t = TOKEN_TILE
    row = lambda n: pl.BlockSpec((t, n), lambda i: (i, 0))
    return pl.pallas_call(
        _gla_proj_kernel,
        grid=(m // t,),
        in_specs=[row(d), _resident(g.shape), _resident(w.shape), _resident(walr.shape),
                  _resident(wa2.shape), _resident(ba2.shape)],
        out_specs=[row(GLA_QK), row(GLA_QK), row(GLA_V), row(GLA_V), row(GLA_QK)],
        out_shape=[jax.ShapeDtypeStruct((m, GLA_QK), BF16), jax.ShapeDtypeStruct((m, GLA_QK), BF16),
                   jax.ShapeDtypeStruct((m, GLA_V), BF16), jax.ShapeDtypeStruct((m, GLA_V), BF16),
                   jax.ShapeDtypeStruct((m, GLA_QK), F32)],
        compiler_params=_params("parallel"),
        name="gla_proj",
    )(x2, g, w, walr, wa2, ba2)


def _dil_proj_kernel(x_ref, pos_ref, g_ref, invf_ref, wq_ref, wk_ref, wv_ref, *out_refs):
    q_refs, k_refs, v_refs = out_refs[0:3], out_refs[3:6], out_refs[6:9]
    a = _rms(x_ref[...], g_ref[...]).astype(BF16)
    ang = pos_ref[...].astype(F32) * invf_ref[...]
    cos, sin = jnp.cos(ang), jnp.sin(ang)
    lane = lax.broadcasted_iota(I32, ang.shape, 1)
    sin_signed = jnp.where(lane < DIL_HEAD_DIM // 2, -sin, sin)

    def rope(t):
        return t * cos + pltpu.roll(t, DIL_HEAD_DIM // 2, 1) * sin_signed

    yq = _dot(a, wq_ref[...])
    yk = _dot(a, wk_ref[...])
    yv = _dot(a, wv_ref[...])
    for g in range(DIL_GROUPS):
        v_refs[g][...] = yv[:, g * DIL_GW:(g + 1) * DIL_GW].astype(BF16)
        for h in range(DIL_HEADS):
            c0 = g * DIL_GW + h * DIL_HEAD_DIM
            hs = slice(h * DIL_HEAD_DIM, (h + 1) * DIL_HEAD_DIM)
            q_refs[g][:, hs] = (rope(yq[:, c0:c0 + DIL_HEAD_DIM]) * (DIL_HEAD_DIM ** -0.5)).astype(BF16)
            k_refs[g][:, hs] = rope(yk[:, c0:c0 + DIL_HEAD_DIM]).astype(BF16)


def _dil_proj(x2, pos2, g, invf, wq, wk, wv):
    m, d = x2.shape
    t = TOKEN_TILE
    row = lambda n: pl.BlockSpec((t, n), lambda i: (i, 0))
    return pl.pallas_call(
        _dil_proj_kernel,
        grid=(m // t,),
        in_specs=[row(d), row(1), _resident(g.shape), _resident(invf.shape),
                  _resident(wq.shape), _resident(wk.shape), _resident(wv.shape)],
        out_specs=[row(DIL_GW)] * 9,
        out_shape=[jax.ShapeDtypeStruct((m, DIL_GW), BF16)] * 9,
        compiler_params=_params("parallel"),
        name="dil_proj",
    )(x2, pos2, g, invf, wq, wk, wv)


def _gla_kernel(q_ref, k_ref, v_ref, r_ref, la_ref, gn_ref, tri_ref, o_ref, st_ref):
    @pl.when(pl.program_id(1) == 0)
    def _():
        st_ref[...] = jnp.zeros_like(st_ref)

    c = GLA_CHUNK
    tri = tri_ref[...]
    causal = lax.broadcasted_iota(I32, (c, c), 0) >= lax.broadcasted_iota(I32, (c, c), 1)
    gn = gn_ref[...]

    def chunk(ci, carry):
        rows = pl.ds(pl.multiple_of(ci * c, c), c)
        q = q_ref[rows, :].astype(F32)
        k = k_ref[rows, :].astype(F32)
        v = v_ref[rows, :]
        r = r_ref[rows, :].astype(F32)
        bcum = jnp.dot(tri, la_ref[rows, :], precision=lax.Precision.HIGHEST,
                       preferred_element_type=F32)
        blast = bcum[c - 1:c, :]
        q_dec = (q * jnp.exp(bcum)).astype(BF16)
        k_inv = (k * jnp.exp(jnp.minimum(-bcum, 80.0))).astype(BF16)
        k_dec = (k * jnp.exp(blast - bcum)).astype(BF16)
        e_last = jnp.exp(blast)
        for h in range(GLA_HEADS):
            ks = slice(h * GLA_DK, (h + 1) * GLA_DK)
            vs = slice(h * GLA_DV, (h + 1) * GLA_DV)
            vh = v[:, vs]
            st = st_ref[h]
            scores = jnp.where(causal, _dot_nt(q_dec[:, ks], k_inv[:, ks]), 0.0)
            o = _dot(scores.astype(BF16), vh) + _dot_nt(q_dec[:, ks], st.astype(BF16))
            v_t = vh.astype(F32).T.astype(BF16)
            st_ref[h] = st * e_last[:, ks] + _dot(v_t, k_dec[:, ks])
            o = o * lax.rsqrt(jnp.mean(o * o, axis=-1, keepdims=True) + EPS) * gn[:, vs]
            o_ref[rows, vs] = (o * r[:, vs]).astype(BF16)
        return carry

    lax.fori_loop(0, q_ref.shape[0] // c, chunk, 0)


def _gla(q, k, v, r, la, gn, batch, seq):
    t = GLA_TILE
    nt = seq // t
    row = lambda n: pl.BlockSpec((t, n), lambda b, j: (b * nt + j, 0))
    tri = jnp.tril(jnp.ones((GLA_CHUNK, GLA_CHUNK), F32))
    return pl.pallas_call(
        _gla_kernel,
        grid=(batch, nt),
        in_specs=[row(GLA_QK), row(GLA_QK), row(GLA_V), row(GLA_V), row(GLA_QK),
                  pl.BlockSpec(gn.shape, lambda b, j: (0, 0)),
                  pl.BlockSpec(tri.shape, lambda b, j: (0, 0))],
        out_specs=row(GLA_V),
        out_shape=jax.ShapeDtypeStruct((batch * seq, GLA_V), BF16),
        scratch_shapes=[pltpu.VMEM((GLA_HEADS, GLA_DV, GLA_DK), F32)],
        compiler_params=_params("parallel", "arbitrary"),
        name="gla",
    )(q, k, v, r, la, gn, tri)


def _dil_attn_kernel(q_ref, k_ref, v_ref, o_ref, lse_ref, kp_ref, vp_ref):
    n = pl.program_id(2)

    @pl.when(n == 0)
    def _():
        kp_ref[...] = jnp.zeros_like(kp_ref)
        vp_ref[...] = jnp.zeros_like(vp_ref)

    qb = DIL_BLOCK
    row = lax.broadcasted_iota(I32, (qb, qb), 0)
    col = lax.broadcasted_iota(I32, (qb, qb), 1)
    bias_cur = jnp.where(col <= row, 0.0, NEG_BIG)
    bias_prev = jnp.where(col >= row, jnp.where(n > 0, 0.0, NEG_BIG), NEG_BIG)
    lane = lax.broadcasted_iota(I32, (qb, LANES), 1)
    lse_all = jnp.zeros((qb, LANES), F32)
    for h in range(DIL_HEADS):
        hs = slice(h * DIL_HEAD_DIM, (h + 1) * DIL_HEAD_DIM)
        q = q_ref[:, hs]
        s_cur = _dot_nt(q, k_ref[:, hs]) + bias_cur
        s_prev = _dot_nt(q, kp_ref[:, hs]) + bias_prev
        m = jnp.maximum(jnp.max(s_cur, axis=-1, keepdims=True), jnp.max(s_prev, axis=-1, keepdims=True))
        p_cur = jnp.exp(s_cur - m)
        p_prev = jnp.exp(s_prev - m)
        l = jnp.sum(p_cur, axis=-1, keepdims=True) + jnp.sum(p_prev, axis=-1, keepdims=True)
        o = _dot(p_cur.astype(BF16), v_ref[:, hs]) + _dot(p_prev.astype(BF16), vp_ref[:, hs])
        o_ref[:, hs] = (o / l).astype(BF16)
        lse_all = jnp.where(lane == h, m + jnp.log(l), lse_all)
    lse_ref[...] = lse_all
    kp_ref[...] = k_ref[...]
    vp_ref[...] = v_ref[...]


def _dil_attn(q, k, v, batch, seq, dilation):
    length = seq // dilation
    nb = length // DIL_BLOCK
    view = lambda t: t.reshape(batch, length, dilation * DIL_GW)
    blk = pl.BlockSpec((None, DIL_BLOCK, DIL_GW), lambda b, p, n: (b, n, p))
    o, lse = pl.pallas_call(
        _dil_attn_kernel,
        grid=(batch, dilation, nb),
        in_specs=[blk, blk, blk],
        out_specs=[blk, pl.BlockSpec((None, DIL_BLOCK, LANES), lambda b, p, n: (b, n, p))],
        out_shape=[jax.ShapeDtypeStruct((batch, length, dilation * DIL_GW), BF16),
                   jax.ShapeDtypeStruct((batch, length, dilation * LANES), F32)],
        scratch_shapes=[pltpu.VMEM((DIL_BLOCK, DIL_GW), BF16), pltpu.VMEM((DIL_BLOCK, DIL_GW), BF16)],
        compiler_params=_params("parallel", "parallel", "arbitrary"),
        name=f"dil_attn_d{dilation}",
    )(view(q), view(k), view(v))
    return o.reshape(batch * seq, DIL_GW), lse.reshape(batch * seq, LANES)


def _mem_kv_kernel(mem_ref, g_ref, w_ref, kv_ref):
    kv_ref[...] = _dot(_rms(mem_ref[...], g_ref[...]).astype(BF16), w_ref[...]).astype(BF16)


def _mem_kv(mem, g, w):
    b, ml, d = mem.shape
    n = w.shape[1]
    return pl.pallas_call(
        _mem_kv_kernel,
        grid=(b,),
        in_specs=[pl.BlockSpec((None, ml, d), lambda i: (i, 0, 0)), _resident(g.shape), _resident(w.shape)],
        out_specs=pl.BlockSpec((None, ml, n), lambda i: (i, 0, 0)),
        out_shape=jax.ShapeDtypeStruct((b, ml, n), BF16),
        compiler_params=_params("parallel"),
        name="mem_kv",
    )(mem, g, w)


def _post_mixer_kernel(x_ref, og_ref, od0_ref, od1_ref, od2_ref, l0_ref, l1_ref, l2_ref, kv_ref,
                       g1_ref, wgate_ref, bgate_ref, wbg_ref, wbd_ref, wmix_ref,
                       g2_ref, wxq_ref, wxo_ref, g3_ref, wr_ref, br_ref,
                       h_ref, f_ref, topi_ref, topg_ref, rank_ref, cnt_ref, carry_ref):
    @pl.when(pl.program_id(0) == 0)
    def _():
        carry_ref[...] = jnp.zeros_like(carry_ref)

    x = x_ref[...]
    t, d = x.shape
    a = _rms(x, g1_ref[...]).astype(BF16)

    l0, l1, l2 = l0_ref[...], l1_ref[...], l2_ref[...]
    lm = jnp.maximum(jnp.maximum(l0, l1), l2)
    e0, e1, e2 = jnp.exp(l0 - lm), jnp.exp(l1 - lm), jnp.exp(l2 - lm)
    inv = 1.0 / (e0 + e1 + e2)
    w0, w1, w2 = e0 * inv, e1 * inv, e2 * inv
    od_heads = []
    for h in range(DIL_HEADS):
        hs = slice(h * DIL_HEAD_DIM, (h + 1) * DIL_HEAD_DIM)
        od_heads.append(w0[:, h:h + 1] * od0_ref[:, hs].astype(F32)
                        + w1[:, h:h + 1] * od1_ref[:, hs].astype(F32)
                        + w2[:, h:h + 1] * od2_ref[:, hs].astype(F32))
    o_d = jnp.concatenate(od_heads, axis=-1).astype(BF16)

    gates = jax.nn.sigmoid(_dot(a, wgate_ref[...]) + bgate_ref[...])
    merged = gates[:, :d] * _dot(og_ref[...], wbg_ref[...]) + gates[:, d:] * _dot(o_d, wbd_ref[...])
    h1 = x + _dot(merged.astype(BF16), wmix_ref[...])

    c = _rms(h1, g2_ref[...]).astype(BF16)
    qx = (_dot(c, wxq_ref[...]) * (XATTN_HEAD_DIM ** -0.5)).astype(BF16)
    xw = XATTN_HEADS * XATTN_HEAD_DIM
    heads = []
    for h in range(XATTN_HEADS):
        hs = slice(h * XATTN_HEAD_DIM, (h + 1) * XATTN_HEAD_DIM)
        vs = slice(xw + h * XATTN_HEAD_DIM, xw + (h + 1) * XATTN_HEAD_DIM)
        s = _dot_nt(qx[:, hs], kv_ref[:, hs])
        p = jnp.exp(s - jnp.max(s, axis=-1, keepdims=True))
        o = _dot(p.astype(BF16), kv_ref[:, vs]) / jnp.sum(p, axis=-1, keepdims=True)
        heads.append(o.astype(BF16))
    h2 = h1 + _dot(jnp.concatenate(heads, axis=-1), wxo_ref[...])
    h_ref[...] = h2

    f = _rms(h2, g3_ref[...])
    f_ref[...] = f
    logits = jnp.dot(f, wr_ref[...], precision=lax.Precision.HIGHEST,
                     preferred_element_type=F32) + br_ref[...]
    lane = lax.broadcasted_iota(I32, (t, LANES), 1)
    work = jnp.where(lane < N_EXPERTS, logits, NEG_BIG)
    selected = jnp.zeros((t, LANES), F32)
    top_val, top_idx = [], []
    for _ in range(TOP_K):
        mx = jnp.max(work, axis=-1, keepdims=True)
        idx = jnp.min(jnp.where(work == mx, lane, LANES), axis=-1, keepdims=True)
        hit = lane == idx
        selected = jnp.where(hit, 1.0, selected)
        work = jnp.where(hit, NEG_BIG, work)
        top_val.append(mx)
        top_idx.append(idx)
    ex = [jnp.exp(v - top_val[0]) for v in top_val]
    inv_den = 1.0 / (ex[0] + ex[1] + ex[2] + ex[3])

    earlier = lax.broadcasted_iota(I32, (t, t), 1) < lax.broadcasted_iota(I32, (t, t), 0)
    before = _dot(jnp.where(earlier, 1.0, 0.0).astype(BF16), selected.astype(BF16)) + carry_ref[0:1, :]
    topi = jnp.zeros((t, LANES), I32)
    topg = jnp.zeros((t, LANES), F32)
    rank = jnp.zeros((t, LANES), I32)
    for kk in range(TOP_K):
        rk = jnp.sum(jnp.where(lane == top_idx[kk], before, 0.0), axis=-1, keepdims=True)
        topi = jnp.where(lane == kk, top_idx[kk], topi)
        topg = jnp.where(lane == kk, ex[kk] * inv_den, topg)
        rank = jnp.where(lane == kk, rk.astype(I32), rank)
    topi_ref[...] = topi
    topg_ref[...] = topg
    rank_ref[...] = rank
    total = carry_ref[0:1, :] + jnp.sum(selected, axis=0, keepdims=True)
    carry_ref[...] = jnp.broadcast_to(total, carry_ref.shape)
    cnt_ref[...] = jnp.broadcast_to(total, cnt_ref.shape)


def _post_mixer(x2, og, ods, lses, kv, seq, g1, wgate, bgate, wbg, wbd, wmix, g2, wxq, wxo, g3, wr, br):
    m, d = x2.shape
    t = POST_TILE
    row = lambda n: pl.BlockSpec((t, n), lambda i: (i, 0))
    kv_spec = pl.BlockSpec((None,) + kv.shape[1:], lambda i: ((i * t) // seq, 0, 0))
    weights = [g1, wgate, bgate, wbg, wbd, wmix, g2, wxq, wxo, g3, wr, br]
    return pl.pallas_call(
        _post_mixer_kernel,
        grid=(m // t,),
        in_specs=[row(d), row(GLA_V), row(DIL_GW), row(DIL_GW), row(DIL_GW),
                  row(LANES), row(LANES), row(LANES), kv_spec] + [_resident(w.shape) for w in weights],
        out_specs=[row(d), row(d), row(LANES), row(LANES), row(LANES),
                   pl.BlockSpec((8, LANES), lambda i: (0, 0))],
        out_shape=[jax.ShapeDtypeStruct((m, d), F32), jax.ShapeDtypeStruct((m, d), F32),
                   jax.ShapeDtypeStruct((m, LANES), I32), jax.ShapeDtypeStruct((m, LANES), F32),
                   jax.ShapeDtypeStruct((m, LANES), I32), jax.ShapeDtypeStruct((8, LANES), F32)],
        scratch_shapes=[pltpu.VMEM((8, LANES), F32)],
        compiler_params=_params("arbitrary"),
        name="post_mixer",
    )(x2, og, *ods, *lses, kv, *weights)


def _row_copy(src_hbm, dst_vmem, sem, src_row, dst_row):
    return pltpu.make_async_copy(src_hbm.at[pl.ds(src_row, 1), :], dst_vmem.at[pl.ds(dst_row, 1), :], sem)


def _experts_kernel(be_ref, nu_ref, idx_ref, idx_next_ref, f_hbm, wg_ref, bg_ref, wu_ref, bu_ref,
                    wd_ref, bd_ref, y_ref, xbuf, sems):
    i = pl.program_id(0)
    n_used = nu_ref[0]
    slot = lax.rem(i, 2)
    rows = xbuf.shape[1]

    def start_gather(index_ref, s):
        def body(r, carry):
            _row_copy(f_hbm, xbuf.at[s], sems.at[s], index_ref[0, 0, r], r).start()
            return carry
        lax.fori_loop(0, rows, body, 0)

    @pl.when((i == 0) & (n_used > 0))
    def _():
        start_gather(idx_ref, 0)

    @pl.when(i + 1 < n_used)
    def _():
        start_gather(idx_next_ref, 1 - slot)

    @pl.when(i < n_used)
    def _():
        def body(r, carry):
            _row_copy(f_hbm, xbuf.at[slot], sems.at[slot], 0, r).wait()
            return carry
        lax.fori_loop(0, rows, body, 0)
        xb = xbuf[slot].astype(BF16)
        glu = jnp.minimum(_dot(xb, wg_ref[...]) + bg_ref[...], SWIGLU_LIMIT)
        lin = jnp.clip(_dot(xb, wu_ref[...]) + bu_ref[...], -SWIGLU_LIMIT, SWIGLU_LIMIT)
        act = glu * jax.nn.sigmoid(SWIGLU_ALPHA * glu) * (lin + 1.0)
        y_ref[...] = _dot(act.astype(BF16), wd_ref[...]) + bd_ref[...]

    @pl.when(i >= n_used)
    def _():
        y_ref[...] = jnp.zeros_like(y_ref)


def _experts(block_e, n_used, row_tok, f, wg, bg, wu, bu, wd, bd):
    m, d = f.shape
    rows = MOE_ROWS
    n_blocks = row_tok.shape[0] // rows
    idx3 = row_tok.reshape(n_blocks, 1, rows)
    smem_blk = lambda fn: pl.BlockSpec((1, 1, rows), fn, memory_space=pltpu.SMEM)
    wspec = lambda w: pl.BlockSpec((None,) + w.shape[1:], lambda i, be, nu: (be[i], 0, 0))
    return pl.pallas_call(
        _experts_kernel,
        grid_spec=pltpu.PrefetchScalarGridSpec(
            num_scalar_prefetch=2,
            grid=(n_blocks,),
            in_specs=[smem_blk(lambda i, be, nu: (i, 0, 0)),
                      smem_blk(lambda i, be, nu: (jnp.minimum(i + 1, n_blocks - 1), 0, 0)),
                      pl.BlockSpec(memory_space=pl.ANY),
                      wspec(wg), wspec(bg), wspec(wu), wspec(bu), wspec(wd), wspec(bd)],
            out_specs=pl.BlockSpec((rows, d), lambda i, be, nu: (i, 0)),
            scratch_shapes=[pltpu.VMEM((2, rows, d), F32), pltpu.SemaphoreType.DMA((2,))],
        ),
        out_shape=jax.ShapeDtypeStruct((n_blocks * rows, d), F32),
        compiler_params=_params("arbitrary"),
        name="experts",
    )(block_e, n_used, idx3, idx3, f, wg, bg, wu, bu, wd, bd)


def _combine_kernel(idx_ref, idx_next_ref, y_hbm, h_ref, topg_ref, g_ref, o_ref, ybuf, sems, *, final):
    i = pl.program_id(0)
    n = pl.num_programs(0)
    slot = lax.rem(i, 2)
    t = h_ref.shape[0]

    def start_gather(index_ref, s):
        def body(r, carry):
            for kk in range(TOP_K):
                _row_copy(y_hbm, ybuf.at[s, kk], sems.at[s], index_ref[0, 0, r * TOP_K + kk], r).start()
            return carry
        lax.fori_loop(0, t, body, 0)

    @pl.when(i == 0)
    def _():
        start_gather(idx_ref, 0)

    @pl.when(i + 1 < n)
    def _():
        start_gather(idx_next_ref, 1 - slot)

    def wait_body(r, carry):
        for kk in range(TOP_K):
            _row_copy(y_hbm, ybuf.at[slot, kk], sems.at[slot], 0, r).wait()
        return carry
    lax.fori_loop(0, t, wait_body, 0)

    gate = topg_ref[...]
    acc = h_ref[...]
    for kk in range(TOP_K):
        acc = acc + gate[:, kk:kk + 1] * ybuf[slot, kk]
    o_ref[...] = _rms(acc, g_ref[...]) if final else acc


def _combine(dest, y, h, topg, g, final):
    m, d = h.shape
    t = COMBINE_TILE
    nt = m // t
    idx3 = dest.reshape(nt, 1, t * TOP_K)
    smem_blk = lambda fn: pl.BlockSpec((1, 1, t * TOP_K), fn, memory_space=pltpu.SMEM)
    row = lambda n: pl.BlockSpec((t, n), lambda i: (i, 0))
    return pl.pallas_call(
        functools.partial(_combine_kernel, final=final),
        grid=(nt,),
        in_specs=[smem_blk(lambda i: (i, 0, 0)), smem_blk(lambda i: (jnp.minimum(i + 1, nt - 1), 0, 0)),
                  pl.BlockSpec(memory_space=pl.ANY), row(d), row(LANES),
                  pl.BlockSpec(g.shape, lambda i: (0, 0))],
        out_specs=row(d),
        out_shape=jax.ShapeDtypeStruct((m, d), F32),
        scratch_shapes=[pltpu.VMEM((2, TOP_K, t, d), F32), pltpu.SemaphoreType.DMA((2,))],
        compiler_params=_params("arbitrary"),
        name="combine",
    )(idx3, idx3, y, h, topg, g)


def _pad_lanes(w, axis):
    pad = [(0, 0)] * w.ndim
    pad[axis] = (0, LANES - w.shape[axis])
    return jnp.pad(w, pad)


def kernel(x, mem, positions, norm1_g, w_in, w_gla_a2, b_gla_a2, gla_norm_g, w_branch_gla, w_branch_dil, w_branch_gate, b_branch_gate, w_mix_out, norm2_g, mem_norm_g, w_xq, w_xkv, w_xo, norm3_g, w_router, b_router, w_gate, b_gate, w_up, b_up, w_down, b_down, final_norm_g):
    batch, seq, d = x.shape
    m = batch * seq
    h = x.reshape(m, d)
    pos2 = positions.reshape(m, 1)
    row = lambda v: v.reshape(1, -1)
    half = DIL_HEAD_DIM // 2
    inv_freq = ROPE_THETA ** (-jnp.arange(half, dtype=F32) / half)
    invf = jnp.concatenate([inv_freq, inv_freq]).reshape(1, DIL_HEAD_DIM)
    off_alr = 2 * GLA_QK + 2 * GLA_V
    off_q = off_alr + GLA_GATE_RANK

    for l in range(w_in.shape[0]):
        w_l = w_in[l]
        g1 = row(norm1_g[l])
        q_g, k_g, v_g, r_g, log_a = _gla_proj(
            h, g1, w_l[:, :off_alr].astype(BF16), _pad_lanes(w_l[:, off_alr:off_q], 1).astype(BF16),
            _pad_lanes(w_gla_a2[l], 0).astype(BF16), row(b_gla_a2[l]))
        dil = _dil_proj(h, pos2, g1, invf, w_l[:, off_q:off_q + DIL_W].astype(BF16),
                        w_l[:, off_q + DIL_W:off_q + 2 * DIL_W].astype(BF16),
                        w_l[:, off_q + 2 * DIL_W:].astype(BF16))
        o_g = _gla(q_g, k_g, v_g, r_g, log_a, row(gla_norm_g[l]), batch, seq)
        ods, lses = [], []
        for g, (_, dilation) in enumerate(DIL_PAIRS):
            o_i, lse_i = _dil_attn(dil[g], dil[3 + g], dil[6 + g], batch, seq, dilation)
            ods.append(o_i)
            lses.append(lse_i)
        kv = _mem_kv(mem, row(mem_norm_g[l]), w_xkv[l].astype(BF16))
        h2, f, topi, topg, rank, cnt = _post_mixer(
            h, o_g, ods, lses, kv, seq, g1, w_branch_gate[l].astype(BF16), row(b_branch_gate[l]),
            w_branch_gla[l].astype(BF16), w_branch_dil[l].astype(BF16), w_mix_out[l].astype(BF16),
            row(norm2_g[l]), w_xq[l].astype(BF16), w_xo[l].astype(BF16), row(norm3_g[l]),
            _pad_lanes(w_router[l], 1), _pad_lanes(row(b_router[l]), 1))

        rows = MOE_ROWS
        counts = cnt[0, :N_EXPERTS].astype(I32)
        padded = ((counts + rows - 1) // rows) * rows
        pend = jnp.cumsum(padded)
        pstart = pend - padded
        n_blocks = -(-(m * TOP_K) // rows) + N_EXPERTS
        dest = pstart[topi[:, :TOP_K]] + rank[:, :TOP_K]
        tok = jnp.broadcast_to(jnp.arange(m, dtype=I32)[:, None], (m, TOP_K))
        row_tok = jnp.zeros((n_blocks * rows,), I32).at[dest.reshape(-1)].set(tok.reshape(-1))
        block_e = jnp.minimum(jnp.searchsorted(pend, jnp.arange(n_blocks, dtype=I32) * rows, side='right'),
                              N_EXPERTS - 1).astype(I32)
        n_used = (pend[-1:] // rows).astype(I32)
        y = _experts(block_e, n_used, row_tok, f, w_gate[l].astype(BF16), b_gate[l][:, None, :],
                     w_up[l].astype(BF16), b_up[l][:, None, :], w_down[l].astype(BF16), b_down[l][:, None, :])
        h = _combine(dest, y, h2, topg, row(final_norm_g), final=l == w_in.shape[0] - 1)
    return h.reshape(batch, seq, d)
```

```python
import functools

import numpy as np
import jax
import jax.numpy as jnp
from jax import lax
from jax.experimental import pallas as pl
from jax.experimental.pallas import tpu as pltpu

F32 = jnp.float32
BF16 = jnp.bfloat16
I32 = jnp.int32

EPS = 1e-5
ROPE_THETA = 10000.0
GLA_HEADS = 4
GLA_DK = 64
GLA_DV = 128
GLA_GATE_RANK = 16
GLA_GATE_TEMP = 16.0
GLA_CHUNK = 64
GLA_QK = GLA_HEADS * GLA_DK
GLA_V = GLA_HEADS * GLA_DV
DIL_PAIRS = ((128, 1), (512, 4), (2048, 16))
DIL_GROUPS = 3
DIL_HEADS = 4
DIL_HEAD_DIM = 128
DIL_BLOCK = 128
DIL_GW = DIL_HEADS * DIL_HEAD_DIM
DIL_W = DIL_GROUPS * DIL_GW
XATTN_HEADS = 4
XATTN_HEAD_DIM = 128
N_EXPERTS = 32
TOP_K = 4
SWIGLU_ALPHA = 1.702
SWIGLU_LIMIT = 7.0

LANES = 128
NEG_BIG = -1e30
VMEM_LIMIT = 56 * 1024 * 1024

TOKEN_TILE = 512
POST_TILE = 256
GLA_TILE = 512
MOE_ROWS = 256
COMBINE_TILE = 256


def _params(*sem):
    return pltpu.CompilerParams(dimension_semantics=sem, vmem_limit_bytes=VMEM_LIMIT)


def _resident(shape):
    nd = len(shape)
    return pl.BlockSpec(shape, lambda *_: (0,) * nd, pipeline_mode=pl.Buffered(1))


def _rms(x, g):
    return x * lax.rsqrt(jnp.mean(x * x, axis=-1, keepdims=True) + EPS) * g


def _dot(a, b):
    return jnp.dot(a, b, preferred_element_type=F32)


def _dot_nt(a, b):
    return lax.dot_general(a, b, (((1,), (1,)), ((), ())), preferred_element_type=F32)


def _gla_proj_kernel(x_ref, g_ref, w_ref, walr_ref, wa2_ref, ba2_ref,
                     q_ref, k_ref, v_ref, r_ref, la_ref):
    a = _rms(x_ref[...], g_ref[...]).astype(BF16)
    y = _dot(a, w_ref[...])
    q_ref[...] = (y[:, :GLA_QK] * (GLA_DK ** -0.5)).astype(BF16)
    k_ref[...] = y[:, GLA_QK:2 * GLA_QK].astype(BF16)
    v_ref[...] = y[:, 2 * GLA_QK:2 * GLA_QK + GLA_V].astype(BF16)
    r = y[:, 2 * GLA_QK + GLA_V:]
    r_ref[...] = (r * jax.nn.sigmoid(r)).astype(BF16)
    a_lr = _dot(a, walr_ref[...])
    logit = _dot(a_lr.astype(BF16), wa2_ref[...]) + ba2_ref[...]
    log_sig = jnp.minimum(logit, 0.0) - jnp.log1p(jnp.exp(-jnp.abs(logit)))
    la_ref[...] = log_sig * (1.0 / GLA_GATE_TEMP)


def _gla_proj(x2, g, w, walr, wa2, ba2):
    m, d = x2.shape
    t = TOKEN_TILE
    row = lambda n: pl.BlockSpec((t, n), lambda i: (i, 0))
    return pl.pallas_call(
        _gla_proj_kernel,
        grid=(m // t,),
        in_specs=[row(d), _resident(g.shape), _resident(w.shape), _resident(walr.shape),
                  _resident(wa2.shape), _resident(ba2.shape)],
        out_specs=[row(GLA_QK), row(GLA_QK), row(GLA_V), row(GLA_V), row(GLA_QK)],
        out_shape=[jax.ShapeDtypeStruct((m, GLA_QK), BF16), jax.ShapeDtypeStruct((m, GLA_QK), BF16),
                   jax.ShapeDtypeStruct((m, GLA_V), BF16), jax.ShapeDtypeStruct((m, GLA_V), BF16),
                   jax.ShapeDtypeStruct((m, GLA_QK), F32)],
        compiler_params=_params("parallel"),
        name="gla_proj",
    )(x2, g, w, walr, wa2, ba2)


def _dil_proj_kernel(x_ref, pos_ref, g_ref, invf_ref, wq_ref, wk_ref, wv_ref, *refs):
    q_refs, k_refs, v_refs = refs[0:3], refs[3:6], refs[6:9]
    stage = refs[9:]
    a = _rms(x_ref[...], g_ref[...]).astype(BF16)
    t = a.shape[0]
    ang = pos_ref[...].astype(F32) * invf_ref[...]
    cos, sin = jnp.cos(ang), jnp.sin(ang)
    lane = lax.broadcasted_iota(I32, ang.shape, 1)
    sin_signed = jnp.where(lane < DIL_HEAD_DIM // 2, -sin, sin)

    def rope(v):
        return v * cos + pltpu.roll(v, DIL_HEAD_DIM // 2, 1) * sin_signed

    ys = (_dot(a, wq_ref[...]), _dot(a, wk_ref[...]), _dot(a, wv_ref[...]))
    post = (lambda v: rope(v) * (DIL_HEAD_DIM ** -0.5), rope, lambda v: v)
    for ti, out_refs in enumerate((q_refs, k_refs, v_refs)):
        for g, (_, dil) in enumerate(DIL_PAIRS):
            for h in range(DIL_HEADS):
                c0 = g * DIL_GW + h * DIL_HEAD_DIM
                hs = slice(h * DIL_HEAD_DIM, (h + 1) * DIL_HEAD_DIM)
                val = post[ti](ys[ti][:, c0:c0 + DIL_HEAD_DIM])
                if dil == 1:
                    out_refs[g][0, :, hs] = val.astype(BF16)
                else:
                    buf = stage[ti * (DIL_GROUPS - 1) + g - 1]
                    buf[h] = val
                    for p in range(dil):
                        out_refs[g][p, :, hs] = buf[h, pl.ds(p, t // dil, stride=dil), :].astype(BF16)


def _dil_proj(x2, pos2, g, invf, wq, wk, wv, batch, seq):
    m, d = x2.shape
    t = TOKEN_TILE
    nt = seq // t
    row = lambda n: pl.BlockSpec((t, n), lambda i: (i, 0))
    out_specs, out_shape = [], []
    for _ in range(3):
        for _, dil in DIL_PAIRS:
            out_specs.append(pl.BlockSpec((None, dil, t // dil, DIL_GW), lambda i: (i // nt, 0, i % nt, 0)))
            out_shape.append(jax.ShapeDtypeStruct((batch, dil, seq // dil, DIL_GW), BF16))
    return pl.pallas_call(
        _dil_proj_kernel,
        grid=(m // t,),
        in_specs=[row(d), row(1), _resident(g.shape), _resident(invf.shape),
                  _resident(wq.shape), _resident(wk.shape), _resident(wv.shape)],
        out_specs=out_specs,
        out_shape=out_shape,
        scratch_shapes=[pltpu.VMEM((DIL_HEADS, t, DIL_HEAD_DIM), F32)] * (3 * (DIL_GROUPS - 1)),
        compiler_params=_params("parallel"),
        name="dil_proj",
    )(x2, pos2, g, invf, wq, wk, wv)


def _gla_kernel(q_ref, k_ref, v_ref, r_ref, la_ref, gn_ref, tri_ref, o_ref, st_ref):
    @pl.when(pl.program_id(1) == 0)
    def _():
        st_ref[...] = jnp.zeros_like(st_ref)

    c = GLA_CHUNK
    tri = tri_ref[...]
    causal = lax.broadcasted_iota(I32, (c, c), 0) >= lax.broadcasted_iota(I32, (c, c), 1)
    gn = gn_ref[...]

    def chunk(ci, carry):
        rows = pl.ds(pl.multiple_of(ci * c, c), c)
        q = q_ref[rows, :].astype(F32)
        k = k_ref[rows, :].astype(F32)
        v = v_ref[rows, :]
        r = r_ref[rows, :].astype(F32)
        bcum = jnp.dot(tri, la_ref[rows, :], precision=lax.Precision.HIGHEST,
                       preferred_element_type=F32)
        blast = bcum[c - 1:c, :]
        q_dec = (q * jnp.exp(bcum)).astype(BF16)
        k_inv = (k * jnp.exp(jnp.minimum(-bcum, 80.0))).astype(BF16)
        k_dec = (k * jnp.exp(blast - bcum)).astype(BF16)
        e_last = jnp.exp(blast)
        for h in range(GLA_HEADS):
            ks = slice(h * GLA_DK, (h + 1) * GLA_DK)
            vs = slice(h * GLA_DV, (h + 1) * GLA_DV)
            vh = v[:, vs]
            st = st_ref[h]
            scores = jnp.where(causal, _dot_nt(q_dec[:, ks], k_inv[:, ks]), 0.0)
            o = _dot(scores.astype(BF16), vh) + _dot_nt(q_dec[:, ks], st.astype(BF16))
            v_t = vh.astype(F32).T.astype(BF16)
            st_ref[h] = st * e_last[:, ks] + _dot(v_t, k_dec[:, ks])
            o = o * lax.rsqrt(jnp.mean(o * o, axis=-1, keepdims=True) + EPS) * gn[:, vs]
            o_ref[rows, vs] = (o * r[:, vs]).astype(BF16)
        return carry

    lax.fori_loop(0, q_ref.shape[0] // c, chunk, 0)


def _gla(q, k, v, r, la, gn, batch, seq):
    t = GLA_TILE
    nt = seq // t
    row = lambda n: pl.BlockSpec((t, n), lambda b, j: (b * nt + j, 0))
    tri = jnp.tril(jnp.ones((GLA_CHUNK, GLA_CHUNK), F32))
    return pl.pallas_call(
        _gla_kernel,
        grid=(batch, nt),
        in_specs=[row(GLA_QK), row(GLA_QK), row(GLA_V), row(GLA_V), row(GLA_QK),
                  pl.BlockSpec(gn.shape, lambda b, j: (0, 0)),
                  pl.BlockSpec(tri.shape, lambda b, j: (0, 0))],
        out_specs=row(GLA_V),
        out_shape=jax.ShapeDtypeStruct((batch * seq, GLA_V), BF16),
        scratch_shapes=[pltpu.VMEM((GLA_HEADS, GLA_DV, GLA_DK), F32)],
        compiler_params=_params("parallel", "arbitrary"),
        name="gla",
    )(q, k, v, r, la, gn, tri)


def _dil_attn_kernel(q_ref, k_ref, v_ref, o_ref, lse_ref, kp_ref, vp_ref):
    n = pl.program_id(2)

    @pl.when(n == 0)
    def _():
        kp_ref[...] = jnp.zeros_like(kp_ref)
        vp_ref[...] = jnp.zeros_like(vp_ref)

    qb = DIL_BLOCK
    row = lax.broadcasted_iota(I32, (qb, qb), 0)
    col = lax.broadcasted_iota(I32, (qb, qb), 1)
    bias_cur = jnp.where(col <= row, 0.0, NEG_BIG)
    bias_prev = jnp.where(col >= row, jnp.where(n > 0, 0.0, NEG_BIG), NEG_BIG)
    lane = lax.broadcasted_iota(I32, (qb, LANES), 1)
    lse_all = jnp.zeros((qb, LANES), F32)
    for h in range(DIL_HEADS):
        hs = slice(h * DIL_HEAD_DIM, (h + 1) * DIL_HEAD_DIM)
        q = q_ref[:, hs]
        s_cur = _dot_nt(q, k_ref[:, hs]) + bias_cur
        s_prev = _dot_nt(q, kp_ref[:, hs]) + bias_prev
        m = jnp.maximum(jnp.max(s_cur, axis=-1, keepdims=True), jnp.max(s_prev, axis=-1, keepdims=True))
        p_cur = jnp.exp(s_cur - m)
        p_prev = jnp.exp(s_prev - m)
        l = jnp.sum(p_cur, axis=-1, keepdims=True) + jnp.sum(p_prev, axis=-1, keepdims=True)
        o = _dot(p_cur.astype(BF16), v_ref[:, hs]) + _dot(p_prev.astype(BF16), vp_ref[:, hs])
        o_ref[:, hs] = (o / l).astype(BF16)
        lse_all = jnp.where(lane == h, m + jnp.log(l), lse_all)
    lse_ref[...] = lse_all
    kp_ref[...] = k_ref[...]
    vp_ref[...] = v_ref[...]


def _dil_attn(q, k, v):
    batch, dilation, length, _ = q.shape
    blk = lambda n: pl.BlockSpec((None, None, DIL_BLOCK, n), lambda b, p, i: (b, p, i, 0))
    return pl.pallas_call(
        _dil_attn_kernel,
        grid=(batch, dilation, length // DIL_BLOCK),
        in_specs=[blk(DIL_GW)] * 3,
        out_specs=[blk(DIL_GW), blk(LANES)],
        out_shape=[jax.ShapeDtypeStruct(q.shape, BF16),
                   jax.ShapeDtypeStruct((batch, dilation, length, LANES), F32)],
        scratch_shapes=[pltpu.VMEM((DIL_BLOCK, DIL_GW), BF16), pltpu.VMEM((DIL_BLOCK, DIL_GW), BF16)],
        compiler_params=_params("parallel", "parallel", "arbitrary"),
        name=f"dil_attn_d{dilation}",
    )(q, k, v)


def _mem_kv_kernel(mem_ref, g_ref, w_ref, kv_ref):
    kv_ref[...] = _dot(_rms(mem_ref[...], g_ref[...]).astype(BF16), w_ref[...]).astype(BF16)


def _mem_kv(mem, g, w):
    b, ml, d = mem.shape
    n = w.shape[1]
    return pl.pallas_call(
        _mem_kv_kernel,
        grid=(b,),
        in_specs=[pl.BlockSpec((None, ml, d), lambda i: (i, 0, 0)), _resident(g.shape), _resident(w.shape)],
        out_specs=pl.BlockSpec((None, ml, n), lambda i: (i, 0, 0)),
        out_shape=jax.ShapeDtypeStruct((b, ml, n), BF16),
        compiler_params=_params("parallel"),
        name="mem_kv",
    )(mem, g, w)


def _post_mixer_kernel(x_ref, og_ref, od0_ref, od1_ref, od2_ref, l0_ref, l1_ref, l2_ref, kv_ref,
                       g1_ref, wgate_ref, bgate_ref, wbg_ref, wbd_ref, wmix_ref,
                       g2_ref, wxq_ref, wxo_ref, g3_ref, wr_ref, br_ref,
                       h_ref, f_ref, topi_ref, topg_ref, rank_ref, cnt_ref,
                       carry_ref, o1_buf, o2_buf, l1_buf, l2_buf):
    @pl.when(pl.program_id(0) == 0)
    def _():
        carry_ref[...] = jnp.zeros_like(carry_ref)

    x = x_ref[...]
    t, d = x.shape
    a = _rms(x, g1_ref[...]).astype(BF16)

    for od_ref, l_ref, o_buf, l_buf in ((od1_ref, l1_ref, o1_buf, l1_buf), (od2_ref, l2_ref, o2_buf, l2_buf)):
        dil = od_ref.shape[0]
        for p in range(dil):
            rows = pl.ds(p, t // dil, stride=dil)
            l_buf[rows, :] = l_ref[p]
            for h in range(DIL_HEADS):
                o_buf[h, rows, :] = od_ref[p, :, h * DIL_HEAD_DIM:(h + 1) * DIL_HEAD_DIM].astype(F32)

    l0, l1, l2 = l0_ref[0], l1_buf[...], l2_buf[...]
    lm = jnp.maximum(jnp.maximum(l0, l1), l2)
    e0, e1, e2 = jnp.exp(l0 - lm), jnp.exp(l1 - lm), jnp.exp(l2 - lm)
    inv = 1.0 / (e0 + e1 + e2)
    w0, w1, w2 = e0 * inv, e1 * inv, e2 * inv
    od_heads = []
    for h in range(DIL_HEADS):
        hs = slice(h * DIL_HEAD_DIM, (h + 1) * DIL_HEAD_DIM)
        od_heads.append(w0[:, h:h + 1] * od0_ref[0, :, hs].astype(F32)
                        + w1[:, h:h + 1] * o1_buf[h] + w2[:, h:h + 1] * o2_buf[h])
    o_d = jnp.concatenate(od_heads, axis=-1).astype(BF16)

    gates = jax.nn.sigmoid(_dot(a, wgate_ref[...]) + bgate_ref[...])
    merged = gates[:, :d] * _dot(og_ref[...], wbg_ref[...]) + gates[:, d:] * _dot(o_d, wbd_ref[...])
    h1 = x + _dot(merged.astype(BF16), wmix_ref[...])

    c = _rms(h1, g2_ref[...]).astype(BF16)
    qx = (_dot(c, wxq_ref[...]) * (XATTN_HEAD_DIM ** -0.5)).astype(BF16)
    xw = XATTN_HEADS * XATTN_HEAD_DIM
    heads = []
    for h in range(XATTN_HEADS):
        hs = slice(h * XATTN_HEAD_DIM, (h + 1) * XATTN_HEAD_DIM)
        vs = slice(xw + h * XATTN_HEAD_DIM, xw + (h + 1) * XATTN_HEAD_DIM)
        s = _dot_nt(qx[:, hs], kv_ref[:, hs])
        p = jnp.exp(s - jnp.max(s, axis=-1, keepdims=True))
        o = _dot(p.astype(BF16), kv_ref[:, vs]) / jnp.sum(p, axis=-1, keepdims=True)
        heads.append(o.astype(BF16))
    h2 = h1 + _dot(jnp.concatenate(heads, axis=-1), wxo_ref[...])
    h_ref[...] = h2

    f = _rms(h2, g3_ref[...])
    for j in range(d // LANES):
        f_ref[pl.ds(j, t, stride=d // LANES), :] = f[:, j * LANES:(j + 1) * LANES]
    logits = jnp.dot(f, wr_ref[...], precision=lax.Precision.HIGHEST,
                     preferred_element_type=F32) + br_ref[...]
    lane = lax.broadcasted_iota(I32, (t, LANES), 1)
    work = jnp.where(lane < N_EXPERTS, logits, NEG_BIG)
    selected = jnp.zeros((t, LANES), F32)
    top_val, top_idx = [], []
    for _ in range(TOP_K):
        mx = jnp.max(work, axis=-1, keepdims=True)
        idx = jnp.min(jnp.where(work == mx, lane, LANES), axis=-1, keepdims=True)
        hit = lane == idx
        selected = jnp.where(hit, 1.0, selected)
        work = jnp.where(hit, NEG_BIG, work)
        top_val.append(mx)
        top_idx.append(idx)
    ex = [jnp.exp(v - top_val[0]) for v in top_val]
    inv_den = 1.0 / (ex[0] + ex[1] + ex[2] + ex[3])

    earlier = lax.broadcasted_iota(I32, (t, t), 1) < lax.broadcasted_iota(I32, (t, t), 0)
    before = _dot(jnp.where(earlier, 1.0, 0.0).astype(BF16), selected.astype(BF16)) + carry_ref[0:1, :]
    topi = jnp.zeros((t, LANES), I32)
    topg = jnp.zeros((t, LANES), F32)
    rank = jnp.zeros((t, LANES), I32)
    for kk in range(TOP_K):
        rk = jnp.sum(jnp.where(lane == top_idx[kk], before, 0.0), axis=-1, keepdims=True)
        topi = jnp.where(lane == kk, top_idx[kk], topi)
        topg = jnp.where(lane == kk, ex[kk] * inv_den, topg)
        rank = jnp.where(lane == kk, rk.astype(I32), rank)
    topi_ref[...] = topi
    topg_ref[...] = topg
    rank_ref[...] = rank
    total = carry_ref[0:1, :] + jnp.sum(selected, axis=0, keepdims=True)
    carry_ref[...] = jnp.broadcast_to(total, carry_ref.shape)
    cnt_ref[...] = jnp.broadcast_to(total, cnt_ref.shape)


def _post_mixer(x2, og, ods, lses, kv, seq, g1, wgate, bgate, wbg, wbd, wmix, g2, wxq, wxo, g3, wr, br):
    m, d = x2.shape
    t = POST_TILE
    nt = seq // t
    sub = d // LANES
    row = lambda n: pl.BlockSpec((t, n), lambda i: (i, 0))
    phase = lambda a: pl.BlockSpec((None, a.shape[1], t // a.shape[1], a.shape[3]),
                                   lambda i: (i // nt, 0, i % nt, 0))
    kv_spec = pl.BlockSpec((None,) + kv.shape[1:], lambda i: (i // nt, 0, 0))
    weights = [g1, wgate, bgate, wbg, wbd, wmix, g2, wxq, wxo, g3, wr, br]
    return pl.pallas_call(
        _post_mixer_kernel,
        grid=(m // t,),
        in_specs=[row(d), row(GLA_V)] + [phase(a) for a in ods] + [phase(a) for a in lses] + [kv_spec]
                 + [_resident(w.shape) for w in weights],
        out_specs=[row(d), pl.BlockSpec((t * sub, LANES), lambda i: (i, 0)), row(LANES), row(LANES), row(LANES),
                   pl.BlockSpec((8, LANES), lambda i: (0, 0))],
        out_shape=[jax.ShapeDtypeStruct((m, d), F32), jax.ShapeDtypeStruct((m * sub, LANES), F32),
                   jax.ShapeDtypeStruct((m, LANES), I32), jax.ShapeDtypeStruct((m, LANES), F32),
                   jax.ShapeDtypeStruct((m, LANES), I32), jax.ShapeDtypeStruct((8, LANES), F32)],
        scratch_shapes=[pltpu.VMEM((8, LANES), F32),
                        pltpu.VMEM((DIL_HEADS, t, DIL_HEAD_DIM), F32), pltpu.VMEM((DIL_HEADS, t, DIL_HEAD_DIM), F32),
                        pltpu.VMEM((t, LANES), F32), pltpu.VMEM((t, LANES), F32)],
        compiler_params=_params("arbitrary"),
        name="post_mixer",
    )(x2, og, *ods, *lses, kv, *weights)


ROW_SUB = 8
DMA_UNROLL = 8


def _tile_copy(src, dst, sem, src_tile, dst_tile):
    return pltpu.make_async_copy(src.at[pl.ds(src_tile * ROW_SUB, ROW_SUB), :],
                                 dst.at[pl.ds(dst_tile * ROW_SUB, ROW_SUB), :], sem)


def _wait_tiles(hbm, vmem, sem):
    pltpu.make_async_copy(hbm.at[pl.ds(0, vmem.shape[0]), :], vmem, sem).wait()


def _experts_kernel(be_ref, nu_ref, tok_ref, tok_next_ref, dst_ref, f_hbm, wg_ref, bg_ref, wu_ref, bu_ref,
                    wd_ref, bd_ref, y_hbm, xbuf, ybuf, in_sems, out_sems):
    i = pl.program_id(0)
    n_blocks = pl.num_programs(0) - 1
    n_used = nu_ref[0]
    slot = lax.rem(i, 2)
    rows = xbuf.shape[1] // ROW_SUB

    def start_gather(index_ref, s):
        def body(r, carry):
            _tile_copy(f_hbm, xbuf.at[s], in_sems.at[s], index_ref[0, 0, r], r).start()
            return carry
        lax.fori_loop(0, rows, body, 0, unroll=DMA_UNROLL)

    @pl.when((i == 0) & (n_used > 0))
    def _():
        start_gather(tok_ref, 0)

    @pl.when(i + 1 < n_used)
    def _():
        start_gather(tok_next_ref, 1 - slot)

    @pl.when(i >= 2)
    def _():
        _wait_tiles(y_hbm, ybuf.at[slot], out_sems.at[slot])

    @pl.when(i == n_blocks)
    def _():
        _wait_tiles(y_hbm, ybuf.at[1 - slot], out_sems.at[1 - slot])

    @pl.when(i < n_used)
    def _():
        _wait_tiles(f_hbm, xbuf.at[slot], in_sems.at[slot])
        xb = jnp.concatenate([xbuf[slot, pl.ds(j, rows, stride=ROW_SUB), :] for j in range(ROW_SUB)],
                             axis=-1).astype(BF16)
        glu = jnp.minimum(_dot(xb, wg_ref[...]) + bg_ref[...], SWIGLU_LIMIT)
        lin = jnp.clip(_dot(xb, wu_ref[...]) + bu_ref[...], -SWIGLU_LIMIT, SWIGLU_LIMIT)
        act = glu * jax.nn.sigmoid(SWIGLU_ALPHA * glu) * (lin + 1.0)
        y = _dot(act.astype(BF16), wd_ref[...]) + bd_ref[...]
        for j in range(ROW_SUB):
            ybuf[slot, pl.ds(j, rows, stride=ROW_SUB), :] = y[:, j * LANES:(j + 1) * LANES]

    @pl.when((i >= n_used) & (i < n_blocks))
    def _():
        ybuf[slot] = jnp.zeros(ybuf.shape[1:], F32)

    @pl.when(i < n_blocks)
    def _():
        def body(r, carry):
            _tile_copy(ybuf.at[slot], y_hbm, out_sems.at[slot], r, dst_ref[0, 0, r]).start()
            return carry
        lax.fori_loop(0, rows, body, 0, unroll=DMA_UNROLL)


def _experts(block_e, n_used, row_tok, row_slot, f, wg, bg, wu, bu, wd, bd):
    rows = MOE_ROWS
    n_blocks = row_tok.shape[0] // rows
    d = wg.shape[1]
    smem_blk = lambda fn: pl.BlockSpec((1, 1, rows), fn, memory_space=pltpu.SMEM)
    at = lambda off: (lambda i, be, nu: (jnp.minimum(i + off, n_blocks - 1), 0, 0))
    wspec = lambda w: pl.BlockSpec((None,) + w.shape[1:], lambda i, be, nu: (be[jnp.minimum(i, n_blocks - 1)], 0, 0))
    return pl.pallas_call(
        _experts_kernel,
        grid_spec=pltpu.PrefetchScalarGridSpec(
            num_scalar_prefetch=2,
            grid=(n_blocks + 1,),
            in_specs=[smem_blk(at(0)), smem_blk(at(1)), smem_blk(at(0)),
                      pl.BlockSpec(memory_space=pl.ANY),
                      wspec(wg), wspec(bg), wspec(wu), wspec(bu), wspec(wd), wspec(bd)],
            out_specs=pl.BlockSpec(memory_space=pl.ANY),
            scratch_shapes=[pltpu.VMEM((2, rows * ROW_SUB, LANES), F32), pltpu.VMEM((2, rows * ROW_SUB, LANES), F32),
                            pltpu.SemaphoreType.DMA((2,)), pltpu.SemaphoreType.DMA((2,))],
        ),
        out_shape=jax.ShapeDtypeStruct((n_blocks * rows * ROW_SUB, LANES), F32),
        compiler_params=_params("arbitrary"),
        name="experts",
    )(block_e, n_used, row_tok.reshape(n_blocks, 1, rows), row_tok.reshape(n_blocks, 1, rows),
      row_slot.reshape(n_blocks, 1, rows), f, wg, bg, wu, bu, wd, bd)


def _combine_kernel(y_ref, h_ref, topg_ref, g_ref, o_ref, *, final):
    t, d = h_ref.shape
    gate = topg_ref[...]
    pieces = []
    for j in range(ROW_SUB):
        acc = h_ref[:, j * LANES:(j + 1) * LANES]
        for kk in range(TOP_K):
            acc = acc + gate[:, kk:kk + 1] * y_ref[pl.ds(kk * ROW_SUB + j, t, stride=TOP_K * ROW_SUB), :]
        pieces.append(acc)
    if final:
        ssq = pieces[0] * pieces[0]
        for piece in pieces[1:]:
            ssq = ssq + piece * piece
        scale = lax.rsqrt(jnp.sum(ssq, axis=-1, keepdims=True) * (1.0 / d) + EPS)
        pieces = [piece * scale * g_ref[:, j * LANES:(j + 1) * LANES] for j, piece in enumerate(pieces)]
    for j, piece in enumerate(pieces):
        o_ref[:, j * LANES:(j + 1) * LANES] = piece


def _combine(y, h, topg, g, final):
    m, d = h.shape
    t = COMBINE_TILE
    row = lambda n: pl.BlockSpec((t, n), lambda i: (i, 0))
    return pl.pallas_call(
        functools.partial(_combine_kernel, final=final),
        grid=(m // t,),
        in_specs=[pl.BlockSpec((t * TOP_K * ROW_SUB, LANES), lambda i: (i, 0)), row(d), row(LANES),
                  pl.BlockSpec(g.shape, lambda i: (0, 0))],
        out_specs=row(d),
        out_shape=jax.ShapeDtypeStruct((m, d), F32),
        compiler_params=_params("parallel"),
        name="combine",
    )(y, h, topg, g)


def _pad_lanes(w, axis):
    pad = [(0, 0)] * w.ndim
    pad[axis] = (0, LANES - w.shape[axis])
    return jnp.pad(w, pad)


def kernel(x, mem, positions, norm1_g, w_in, w_gla_a2, b_gla_a2, gla_norm_g, w_branch_gla, w_branch_dil, w_branch_gate, b_branch_gate, w_mix_out, norm2_g, mem_norm_g, w_xq, w_xkv, w_xo, norm3_g, w_router, b_router, w_gate, b_gate, w_up, b_up, w_down, b_down, final_norm_g):
    batch, seq, d = x.shape
    m = batch * seq
    h = x.reshape(m, d)
    pos2 = positions.reshape(m, 1)
    row = lambda v: v.reshape(1, -1)
    half = DIL_HEAD_DIM // 2
    inv_freq = ROPE_THETA ** (-jnp.arange(half, dtype=F32) / half)
    invf = jnp.concatenate([inv_freq, inv_freq]).reshape(1, DIL_HEAD_DIM)
    off_alr = 2 * GLA_QK + 2 * GLA_V
    off_q = off_alr + GLA_GATE_RANK

    for l in range(w_in.shape[0]):
        w_l = w_in[l]
        g1 = row(norm1_g[l])
        q_g, k_g, v_g, r_g, log_a = _gla_proj(
            h, g1, w_l[:, :off_alr].astype(BF16), _pad_lanes(w_l[:, off_alr:off_q], 1).astype(BF16),
            _pad_lanes(w_gla_a2[l], 0).astype(BF16), row(b_gla_a2[l]))
        dil = _dil_proj(h, pos2, g1, invf, w_l[:, off_q:off_q + DIL_W].astype(BF16),
                        w_l[:, off_q + DIL_W:off_q + 2 * DIL_W].astype(BF16),
                        w_l[:, off_q + 2 * DIL_W:].astype(BF16), batch, seq)
        o_g = _gla(q_g, k_g, v_g, r_g, log_a, row(gla_norm_g[l]), batch, seq)
        ods, lses = [], []
        for g in range(DIL_GROUPS):
            o_i, lse_i = _dil_attn(dil[g], dil[DIL_GROUPS + g], dil[2 * DIL_GROUPS + g])
            ods.append(o_i)
            lses.append(lse_i)
        kv = _mem_kv(mem, row(mem_norm_g[l]), w_xkv[l].astype(BF16))
        h2, f, topi, topg, rank, cnt = _post_mixer(
            h, o_g, ods, lses, kv, seq, g1, w_branch_gate[l].astype(BF16), row(b_branch_gate[l]),
            w_branch_gla[l].astype(BF16), w_branch_dil[l].astype(BF16), w_mix_out[l].astype(BF16),
            row(norm2_g[l]), w_xq[l].astype(BF16), w_xo[l].astype(BF16), row(norm3_g[l]),
            _pad_lanes(w_router[l], 1), _pad_lanes(row(b_router[l]), 1))

        rows = MOE_ROWS
        counts = cnt[0, :N_EXPERTS].astype(I32)
        padded = ((counts + rows - 1) // rows) * rows
        pend = jnp.cumsum(padded)
        pstart = pend - padded
        n_assign = m * TOP_K
        n_blocks = -(-n_assign // rows) + N_EXPERTS
        dest = pstart[topi[:, :TOP_K]] + rank[:, :TOP_K]
        slot_of_row = jnp.full((n_blocks * rows,), -1, I32).at[dest.reshape(-1)].set(jnp.arange(n_assign, dtype=I32))
        is_pad = slot_of_row < 0
        row_slot = jnp.where(is_pad, n_assign - 1 + jnp.cumsum(is_pad.astype(I32)), slot_of_row)
        row_tok = jnp.where(is_pad, 0, slot_of_row // TOP_K)
        block_start = jnp.arange(n_blocks, dtype=I32) * rows
        block_e = jnp.minimum(jnp.sum((block_start[:, None] >= pend[None, :]).astype(I32), axis=1), N_EXPERTS - 1)
        n_used = (pend[-1:] // rows).astype(I32)
        y = _experts(block_e, n_used, row_tok, row_slot, f, w_gate[l].astype(BF16), b_gate[l][:, None, :],
                     w_up[l].astype(BF16), b_up[l][:, None, :], w_down[l].astype(BF16), b_down[l][:, None, :])
        h = _combine(y, h2, topg, row(final_norm_g), final=l == w_in.shape[0] - 1)
    return h.reshape(batch, seq, d)
```

```python
import functools

import numpy as np
import jax
import jax.numpy as jnp
from jax import lax
from jax.experimental import pallas as pl
from jax.experimental.pallas import tpu as pltpu

F32 = jnp.float32
BF16 = jnp.bfloat16
I32 = jnp.int32

EPS = 1e-5
ROPE_THETA = 10000.0
GLA_HEADS = 4
GLA_DK = 64
GLA_DV = 128
GLA_GATE_RANK = 16
GLA_GATE_TEMP = 16.0
GLA_CHUNK = 64
GLA_QK = GLA_HEADS * GLA_DK
GLA_V = GLA_HEADS * GLA_DV
DIL_PAIRS = ((128, 1), (512, 4), (2048, 16))
DIL_GROUPS = 3
DIL_HEADS = 4
DIL_HEAD_DIM = 128
DIL_BLOCK = 128
DIL_GW = DIL_HEADS * DIL_HEAD_DIM
DIL_W = DIL_GROUPS * DIL_GW
XATTN_HEADS = 4
XATTN_HEAD_DIM = 128
N_EXPERTS = 32
TOP_K = 4
SWIGLU_ALPHA = 1.702
SWIGLU_LIMIT = 7.0

LANES = 128
NEG_BIG = -1e30
VMEM_LIMIT = 56 * 1024 * 1024

TOKEN_TILE = 512
POST_TILE = 256
GLA_TILE = 512
MOE_ROWS = 256
COMBINE_TILE = 256


def _params(*sem):
    return pltpu.CompilerParams(dimension_semantics=sem, vmem_limit_bytes=VMEM_LIMIT)


def _resident(shape):
    nd = len(shape)
    return pl.BlockSpec(shape, lambda *_: (0,) * nd, pipeline_mode=pl.Buffered(1))


def _rms(x, g):
    return x * lax.rsqrt(jnp.mean(x * x, axis=-1, keepdims=True) + EPS) * g


def _dot(a, b):
    return jnp.dot(a, b, preferred_element_type=F32)


def _dot_nt(a, b):
    return lax.dot_general(a, b, (((1,), (1,)), ((), ())), preferred_element_type=F32)


def _gla_proj_kernel(x_ref, g_ref, w_ref, walr_ref, wa2_ref, ba2_ref,
                     q_ref, k_ref, v_ref, r_ref, la_ref):
    a = _rms(x_ref[...], g_ref[...]).astype(BF16)
    y = _dot(a, w_ref[...])
    q_ref[...] = (y[:, :GLA_QK] * (GLA_DK ** -0.5)).astype(BF16)
    k_ref[...] = y[:, GLA_QK:2 * GLA_QK].astype(BF16)
    v_ref[...] = y[:, 2 * GLA_QK:2 * GLA_QK + GLA_V].astype(BF16)
    r = y[:, 2 * GLA_QK + GLA_V:]
    r_ref[...] = (r * jax.nn.sigmoid(r)).astype(BF16)
    a_lr = _dot(a, walr_ref[...])
    logit = _dot(a_lr.astype(BF16), wa2_ref[...]) + ba2_ref[...]
    log_sig = jnp.minimum(logit, 0.0) - jnp.log1p(jnp.exp(-jnp.abs(logit)))
    la_ref[...] = log_sig * (1.0 / GLA_GATE_TEMP)


def _gla_proj(x2, g, w, walr, wa2, ba2):
    m, d = x2.shape
    t = TOKEN_TILE
    row = lambda n: pl.BlockSpec((t, n), lambda i: (i, 0))
    return pl.pallas_call(
        _gla_proj_kernel,
        grid=(m // t,),
        in_specs=[row(d), _resident(g.shape), _resident(w.shape), _resident(walr.shape),
                  _resident(wa2.shape), _resident(ba2.shape)],
        out_specs=[row(GLA_QK), row(GLA_QK), row(GLA_V), row(GLA_V), row(GLA_QK)],
        out_shape=[jax.ShapeDtypeStruct((m, GLA_QK), BF16), jax.ShapeDtypeStruct((m, GLA_QK), BF16),
                   jax.ShapeDtypeStruct((m, GLA_V), BF16), jax.ShapeDtypeStruct((m, GLA_V), BF16),
                   jax.ShapeDtypeStruct((m, GLA_QK), F32)],
        compiler_params=_params("parallel"),
        name="gla_proj",
    )(x2, g, w, walr, wa2, ba2)


def _dil_proj_kernel(x_ref, pos_ref, g_ref, invf_ref, wq_ref, wk_ref, wv_ref, *refs):
    q_refs, k_refs, v_refs = refs[0:3], refs[3:6], refs[6:9]
    stage = refs[9:]
    a = _rms(x_ref[...], g_ref[...]).astype(BF16)
    t = a.shape[0]
    ang = pos_ref[...].astype(F32) * invf_ref[...]
    cos, sin = jnp.cos(ang), jnp.sin(ang)
    lane = lax.broadcasted_iota(I32, ang.shape, 1)
    sin_signed = jnp.where(lane < DIL_HEAD_DIM // 2, -sin, sin)

    def rope(v):
        return v * cos + pltpu.roll(v, DIL_HEAD_DIM // 2, 1) * sin_signed

    ys = (_dot(a, wq_ref[...]), _dot(a, wk_ref[...]), _dot(a, wv_ref[...]))
    post = (lambda v: rope(v) * (DIL_HEAD_DIM ** -0.5), rope, lambda v: v)
    for ti, out_refs in enumerate((q_refs, k_refs, v_refs)):
        for g, (_, dil) in enumerate(DIL_PAIRS):
            for h in range(DIL_HEADS):
                c0 = g * DIL_GW + h * DIL_HEAD_DIM
                hs = slice(h * DIL_HEAD_DIM, (h + 1) * DIL_HEAD_DIM)
                val = post[ti](ys[ti][:, c0:c0 + DIL_HEAD_DIM])
                if dil == 1:
                    out_refs[g][0, :, hs] = val.astype(BF16)
                else:
                    buf = stage[ti * (DIL_GROUPS - 1) + g - 1]
                    buf[h] = val
                    for p in range(dil):
                        out_refs[g][p, :, hs] = buf[h, pl.ds(p, t // dil, stride=dil), :].astype(BF16)


def _dil_proj(x2, pos2, g, invf, wq, wk, wv, batch, seq):
    m, d = x2.shape
    t = TOKEN_TILE
    nt = seq // t
    row = lambda n: pl.BlockSpec((t, n), lambda i: (i, 0))
    out_specs, out_shape = [], []
    for _ in range(3):
        for _, dil in DIL_PAIRS:
            out_specs.append(pl.BlockSpec((None, dil, t // dil, DIL_GW), lambda i: (i // nt, 0, i % nt, 0)))
            out_shape.append(jax.ShapeDtypeStruct((batch, dil, seq // dil, DIL_GW), BF16))
    return pl.pallas_call(
        _dil_proj_kernel,
        grid=(m // t,),
        in_specs=[row(d), row(1), _resident(g.shape), _resident(invf.shape),
                  _resident(wq.shape), _resident(wk.shape), _resident(wv.shape)],
        out_specs=out_specs,
        out_shape=out_shape,
        scratch_shapes=[pltpu.VMEM((DIL_HEADS, t, DIL_HEAD_DIM), F32)] * (3 * (DIL_GROUPS - 1)),
        compiler_params=_params("parallel"),
        name="dil_proj",
    )(x2, pos2, g, invf, wq, wk, wv)


GLA_LEVELS = (32, 16, 8, 4, 2, 1)
GLA_ONE_FACTOR_MIN = -60.0


def _gla_level_masks():
    c = GLA_CHUNK
    t = np.arange(c)[:, None]
    r = np.arange(c)[None, :]
    blocks = []
    for h in GLA_LEVELS:
        u = t % (2 * h)
        m = t - u + h - 1
        blocks.append(((u >= h) & (r > m) & (r <= t)).astype(np.float32))
        blocks.append(((u < h) & (r > t) & (r <= m)).astype(np.float32))
    return np.concatenate(blocks, axis=0)


def _gla_kernel(q_ref, k_ref, v_ref, r_ref, la_ref, gn_ref, tri_ref, lvl_ref, o_ref, st_ref, bc_ref):
    @pl.when(pl.program_id(1) == 0)
    def _():
        st_ref[...] = jnp.zeros_like(st_ref)

    c = GLA_CHUNK
    n_chunks = q_ref.shape[0] // c
    row_i = lax.broadcasted_iota(I32, (c, c), 0)
    col_i = lax.broadcasted_iota(I32, (c, c), 1)
    gn = gn_ref[...]

    tri = tri_ref[...]
    min_last = None
    for ci in range(n_chunks):
        rows = slice(ci * c, (ci + 1) * c)
        bcum = jnp.dot(tri, la_ref[rows, :], precision=lax.Precision.HIGHEST,
                       preferred_element_type=F32)
        bc_ref[rows, :] = bcum
        last = bcum[c - 1:c, :]
        min_last = last if min_last is None else jnp.minimum(min_last, last)
    one_factor_ok = jnp.min(min_last) >= GLA_ONE_FACTOR_MIN

    def finish_chunk(rows, q_dec, k_dec, e_last, v, r, scores_of_head):
        for h in range(GLA_HEADS):
            ks = slice(h * GLA_DK, (h + 1) * GLA_DK)
            vs = slice(h * GLA_DV, (h + 1) * GLA_DV)
            vh = v[:, vs]
            st = st_ref[h]
            o = _dot(scores_of_head(h, ks).astype(BF16), vh) + _dot_nt(q_dec[:, ks], st.astype(BF16))
            v_t = vh.astype(F32).T.astype(BF16)
            st_ref[h] = st * e_last[:, ks] + _dot(v_t, k_dec[:, ks])
            o = o * lax.rsqrt(jnp.mean(o * o, axis=-1, keepdims=True) + EPS) * gn[:, vs]
            o_ref[rows, vs] = (o * r[:, vs]).astype(BF16)

    def load_chunk(rows):
        q = q_ref[rows, :].astype(F32)
        k = k_ref[rows, :].astype(F32)
        bcum = bc_ref[rows, :]
        blast = bcum[c - 1:c, :]
        q_dec = (q * jnp.exp(bcum)).astype(BF16)
        k_dec = (k * jnp.exp(blast - bcum)).astype(BF16)
        return q, k, bcum, q_dec, k_dec, jnp.exp(blast), v_ref[rows, :], r_ref[rows, :].astype(F32)

    @pl.when(one_factor_ok)
    def _():
        for ci in range(n_chunks):
            rows = slice(ci * c, (ci + 1) * c)
            q, k, bcum, q_dec, k_dec, e_last, v, r = load_chunk(rows)
            k_inv = (k * jnp.exp(-bcum)).astype(BF16)
            finish_chunk(rows, q_dec, k_dec, e_last, v, r,
                         lambda h, ks: jnp.where(row_i >= col_i, _dot_nt(q_dec[:, ks], k_inv[:, ks]), 0.0))

    @pl.when(jnp.logical_not(one_factor_ok))
    def _():
        def chunk(ci, carry):
            rows = pl.ds(pl.multiple_of(ci * c, c), c)
            q, k, bcum, q_dec, k_dec, e_last, v, r = load_chunk(rows)
            expo = jnp.dot(lvl_ref[...], la_ref[rows, :], precision=lax.Precision.HIGHEST,
                           preferred_element_type=F32)
            q_lv = [(q * jnp.exp(expo[2 * j * c:(2 * j + 1) * c, :])).astype(BF16) for j in range(len(GLA_LEVELS))]
            k_lv = [(k * jnp.exp(expo[(2 * j + 1) * c:(2 * j + 2) * c, :])).astype(BF16) for j in range(len(GLA_LEVELS))]
            q_bf, k_bf = q.astype(BF16), k.astype(BF16)

            def scores_of_head(h, ks):
                sc = jnp.where(row_i == col_i, _dot_nt(q_bf[:, ks], k_bf[:, ks]), 0.0)
                differ = row_i ^ col_i
                for j, half in enumerate(GLA_LEVELS):
                    valid = (differ >= half) & (differ < 2 * half) & ((row_i & half) != 0)
                    sc = sc + jnp.where(valid, _dot_nt(q_lv[j][:, ks], k_lv[j][:, ks]), 0.0)
                return sc
            finish_chunk(rows, q_dec, k_dec, e_last, v, r, scores_of_head)
            return carry
        lax.fori_loop(0, n_chunks, chunk, 0)


def _gla(q, k, v, r, la, gn, batch, seq):
    t = GLA_TILE
    nt = seq // t
    row = lambda n: pl.BlockSpec((t, n), lambda b, j: (b * nt + j, 0))
    const = lambda a: pl.BlockSpec(a.shape, lambda b, j: (0, 0))
    tri = jnp.tril(jnp.ones((GLA_CHUNK, GLA_CHUNK), F32))
    lvl = jnp.asarray(_gla_level_masks())
    return pl.pallas_call(
        _gla_kernel,
        grid=(batch, nt),
        in_specs=[row(GLA_QK), row(GLA_QK), row(GLA_V), row(GLA_V), row(GLA_QK), const(gn), const(tri), const(lvl)],
        out_specs=row(GLA_V),
        out_shape=jax.ShapeDtypeStruct((batch * seq, GLA_V), BF16),
        scratch_shapes=[pltpu.VMEM((GLA_HEADS, GLA_DV, GLA_DK), F32), pltpu.VMEM((t, GLA_QK), F32)],
        compiler_params=_params("parallel", "arbitrary"),
        name="gla",
    )(q, k, v, r, la, gn, tri, lvl)


def _dil_attn_kernel(q_ref, k_ref, v_ref, o_ref, lse_ref, kp_ref, vp_ref):
    n = pl.program_id(2)

    @pl.when(n == 0)
    def _():
        kp_ref[...] = jnp.zeros_like(kp_ref)
        vp_ref[...] = jnp.zeros_like(vp_ref)

    qb = DIL_BLOCK
    row = lax.broadcasted_iota(I32, (qb, qb), 0)
    col = lax.broadcasted_iota(I32, (qb, qb), 1)
    lower = col <= row
    prev_bias = jnp.where(n > 0, 0.0, NEG_BIG)
    lane = lax.broadcasted_iota(I32, (qb, LANES), 1)
    lse_all = jnp.zeros((qb, LANES), F32)
    for h in range(DIL_HEADS):
        hs = slice(h * DIL_HEAD_DIM, (h + 1) * DIL_HEAD_DIM)
        q = q_ref[:, hs]
        s_cur = _dot_nt(q, k_ref[:, hs])
        s_prev = _dot_nt(q, kp_ref[:, hs])
        s = jnp.where(lower, s_cur, s_prev + prev_bias)
        s_far = jnp.sum(jnp.where(row == col, s_prev, 0.0), axis=-1, keepdims=True) + prev_bias
        m = jnp.maximum(jnp.max(s, axis=-1, keepdims=True), s_far)
        p = jnp.exp(s - m)
        p_far = jnp.exp(s_far - m)
        l = jnp.sum(p, axis=-1, keepdims=True) + p_far
        vp = vp_ref[:, hs]
        o = (_dot(jnp.where(lower, p, 0.0).astype(BF16), v_ref[:, hs])
             + _dot(jnp.where(lower, 0.0, p).astype(BF16), vp) + p_far * vp.astype(F32))
        o_ref[:, hs] = (o / l).astype(BF16)
        lse_all = jnp.where(lane == h, m + jnp.log(l), lse_all)
    lse_ref[...] = lse_all
    kp_ref[...] = k_ref[...]
    vp_ref[...] = v_ref[...]


def _dil_attn(q, k, v):
    batch, dilation, length, _ = q.shape
    blk = lambda n: pl.BlockSpec((None, None, DIL_BLOCK, n), lambda b, p, i: (b, p, i, 0))
    return pl.pallas_call(
        _dil_attn_kernel,
        grid=(batch, dilation, length // DIL_BLOCK),
        in_specs=[blk(DIL_GW)] * 3,
        out_specs=[blk(DIL_GW), blk(LANES)],
        out_shape=[jax.ShapeDtypeStruct(q.shape, BF16),
                   jax.ShapeDtypeStruct((batch, dilation, length, LANES), F32)],
        scratch_shapes=[pltpu.VMEM((DIL_BLOCK, DIL_GW), BF16), pltpu.VMEM((DIL_BLOCK, DIL_GW), BF16)],
        compiler_params=_params("parallel", "parallel", "arbitrary"),
        name=f"dil_attn_d{dilation}",
    )(q, k, v)


def _mem_kv_kernel(mem_ref, g_ref, w_ref, kv_ref):
    kv_ref[...] = _dot(_rms(mem_ref[...], g_ref[...]).astype(BF16), w_ref[...]).astype(BF16)


def _mem_kv(mem, g, w):
    b, ml, d = mem.shape
    n = w.shape[1]
    return pl.pallas_call(
        _mem_kv_kernel,
        grid=(b,),
        in_specs=[pl.BlockSpec((None, ml, d), lambda i: (i, 0, 0)), _resident(g.shape), _resident(w.shape)],
        out_specs=pl.BlockSpec((None, ml, n), lambda i: (i, 0, 0)),
        out_shape=jax.ShapeDtypeStruct((b, ml, n), BF16),
        compiler_params=_params("parallel"),
        name="mem_kv",
    )(mem, g, w)


def _post_mixer_kernel(x_ref, og_ref, od0_ref, od1_ref, od2_ref, l0_ref, l1_ref, l2_ref, kv_ref,
                       g1_ref, wgate_ref, bgate_ref, wbg_ref, wbd_ref, wmix_ref,
                       g2_ref, wxq_ref, wxo_ref, g3_ref, wr_ref, br_ref,
                       h_ref, f_ref, topi_ref, topg_ref, rank_ref, cnt_ref,
                       carry_ref, o1_buf, o2_buf, l1_buf, l2_buf):
    @pl.when(pl.program_id(0) == 0)
    def _():
        carry_ref[...] = jnp.zeros_like(carry_ref)

    x = x_ref[...]
    t, d = x.shape
    a = _rms(x, g1_ref[...]).astype(BF16)

    for od_ref, l_ref, o_buf, l_buf in ((od1_ref, l1_ref, o1_buf, l1_buf), (od2_ref, l2_ref, o2_buf, l2_buf)):
        dil = od_ref.shape[0]
        for p in range(dil):
            rows = pl.ds(p, t // dil, stride=dil)
            l_buf[rows, :] = l_ref[p]
            for h in range(DIL_HEADS):
                o_buf[h, rows, :] = od_ref[p, :, h * DIL_HEAD_DIM:(h + 1) * DIL_HEAD_DIM].astype(F32)

    l0, l1, l2 = l0_ref[0], l1_buf[...], l2_buf[...]
    lm = jnp.maximum(jnp.maximum(l0, l1), l2)
    e0, e1, e2 = jnp.exp(l0 - lm), jnp.exp(l1 - lm), jnp.exp(l2 - lm)
    inv = 1.0 / (e0 + e1 + e2)
    w0, w1, w2 = e0 * inv, e1 * inv, e2 * inv
    od_heads = []
    for h in range(DIL_HEADS):
        hs = slice(h * DIL_HEAD_DIM, (h + 1) * DIL_HEAD_DIM)
        od_heads.append(w0[:, h:h + 1] * od0_ref[0, :, hs].astype(F32)
                        + w1[:, h:h + 1] * o1_buf[h] + w2[:, h:h + 1] * o2_buf[h])
    o_d = jnp.concatenate(od_heads, axis=-1).astype(BF16)

    gates = jax.nn.sigmoid(_dot(a, wgate_ref[...]) + bgate_ref[...])
    merged = gates[:, :d] * _dot(og_ref[...], wbg_ref[...]) + gates[:, d:] * _dot(o_d, wbd_ref[...])
    h1 = x + _dot(merged.astype(BF16), wmix_ref[...])

    c = _rms(h1, g2_ref[...]).astype(BF16)
    qx = (_dot(c, wxq_ref[...]) * (XATTN_HEAD_DIM ** -0.5)).astype(BF16)
    xw = XATTN_HEADS * XATTN_HEAD_DIM
    heads = []
    for h in range(XATTN_HEADS):
        hs = slice(h * XATTN_HEAD_DIM, (h + 1) * XATTN_HEAD_DIM)
        vs = slice(xw + h * XATTN_HEAD_DIM, xw + (h + 1) * XATTN_HEAD_DIM)
        s = _dot_nt(qx[:, hs], kv_ref[:, hs])
        p = jnp.exp(s - jnp.max(s, axis=-1, keepdims=True))
        o = _dot(p.astype(BF16), kv_ref[:, vs]) / jnp.sum(p, axis=-1, keepdims=True)
        heads.append(o.astype(BF16))
    h2 = h1 + _dot(jnp.concatenate(heads, axis=-1), wxo_ref[...])
    h_ref[...] = h2

    f = _rms(h2, g3_ref[...])
    for j in range(d // LANES):
        f_ref[pl.ds(j, t, stride=d // LANES), :] = f[:, j * LANES:(j + 1) * LANES]
    logits = jnp.dot(f, wr_ref[...], precision=lax.Precision.HIGHEST,
                     preferred_element_type=F32) + br_ref[...]
    lane = lax.broadcasted_iota(I32, (t, LANES), 1)
    work = jnp.where(lane < N_EXPERTS, logits, NEG_BIG)
    selected = jnp.zeros((t, LANES), F32)
    top_val, top_idx = [], []
    for _ in range(TOP_K):
        mx = jnp.max(work, axis=-1, keepdims=True)
        idx = jnp.min(jnp.where(work == mx, lane, LANES), axis=-1, keepdims=True)
        hit = lane == idx
        selected = jnp.where(hit, 1.0, selected)
        work = jnp.where(hit, NEG_BIG, work)
        top_val.append(mx)
        top_idx.append(idx)
    ex = [jnp.exp(v - top_val[0]) for v in top_val]
    inv_den = 1.0 / (ex[0] + ex[1] + ex[2] + ex[3])

    earlier = lax.broadcasted_iota(I32, (t, t), 1) < lax.broadcasted_iota(I32, (t, t), 0)
    before = _dot(jnp.where(earlier, 1.0, 0.0).astype(BF16), selected.astype(BF16)) + carry_ref[0:1, :]
    topi = jnp.zeros((t, LANES), I32)
    topg = jnp.zeros((t, LANES), F32)
    rank = jnp.zeros((t, LANES), I32)
    for kk in range(TOP_K):
        rk = jnp.sum(jnp.where(lane == top_idx[kk], before, 0.0), axis=-1, keepdims=True)
        topi = jnp.where(lane == kk, top_idx[kk], topi)
        topg = jnp.where(lane == kk, ex[kk] * inv_den, topg)
        rank = jnp.where(lane == kk, rk.astype(I32), rank)
    topi_ref[...] = topi
    topg_ref[...] = topg
    rank_ref[...] = rank
    total = carry_ref[0:1, :] + jnp.sum(selected, axis=0, keepdims=True)
    carry_ref[...] = jnp.broadcast_to(total, carry_ref.shape)
    cnt_ref[...] = jnp.broadcast_to(total, cnt_ref.shape)


def _post_mixer(x2, og, ods, lses, kv, seq, g1, wgate, bgate, wbg, wbd, wmix, g2, wxq, wxo, g3, wr, br):
    m, d = x2.shape
    t = POST_TILE
    nt = seq // t
    sub = d // LANES
    row = lambda n: pl.BlockSpec((t, n), lambda i: (i, 0))
    phase = lambda a: pl.BlockSpec((None, a.shape[1], t // a.shape[1], a.shape[3]),
                                   lambda i: (i // nt, 0, i % nt, 0))
    kv_spec = pl.BlockSpec((None,) + kv.shape[1:], lambda i: (i // nt, 0, 0))
    weights = [g1, wgate, bgate, wbg, wbd, wmix, g2, wxq, wxo, g3, wr, br]
    return pl.pallas_call(
        _post_mixer_kernel,
        grid=(m // t,),
        in_specs=[row(d), row(GLA_V)] + [phase(a) for a in ods] + [phase(a) for a in lses] + [kv_spec]
                 + [_resident(w.shape) for w in weights],
        out_specs=[row(d), pl.BlockSpec((t * sub, LANES), lambda i: (i, 0)), row(LANES), row(LANES), row(LANES),
                   pl.BlockSpec((8, LANES), lambda i: (0, 0))],
        out_shape=[jax.ShapeDtypeStruct((m, d), F32), jax.ShapeDtypeStruct((m * sub, LANES), F32),
                   jax.ShapeDtypeStruct((m, LANES), I32), jax.ShapeDtypeStruct((m, LANES), F32),
                   jax.ShapeDtypeStruct((m, LANES), I32), jax.ShapeDtypeStruct((8, LANES), F32)],
        scratch_shapes=[pltpu.VMEM((8, LANES), F32),
                        pltpu.VMEM((DIL_HEADS, t, DIL_HEAD_DIM), F32), pltpu.VMEM((DIL_HEADS, t, DIL_HEAD_DIM), F32),
                        pltpu.VMEM((t, LANES), F32), pltpu.VMEM((t, LANES), F32)],
        compiler_params=_params("arbitrary"),
        name="post_mixer",
    )(x2, og, *ods, *lses, kv, *weights)


ROW_SUB = 8
DMA_UNROLL = 8


def _tile_copy(src, dst, sem, src_tile, dst_tile):
    return pltpu.make_async_copy(src.at[pl.ds(src_tile * ROW_SUB, ROW_SUB), :],
                                 dst.at[pl.ds(dst_tile * ROW_SUB, ROW_SUB), :], sem)


def _wait_tiles(hbm, vmem, sem):
    pltpu.make_async_copy(hbm.at[pl.ds(0, vmem.shape[0]), :], vmem, sem).wait()


EXPERT_COL_CHUNKS = 4


def _experts_kernel(be_ref, nu_ref, tok_ref, tok_next_ref, dst_prev_ref, f_hbm, wg_ref, bg_ref, wu_ref, bu_ref,
                    wd_ref, bd_ref, y_hbm, xbuf, ybuf, in_sems, out_sems):
    i = pl.program_id(0)
    n_blocks = pl.num_programs(0) - 2
    n_used = nu_ref[0]
    xs, xs_next = lax.rem(i, 2), lax.rem(i + 1, 2)
    ys, ys_prev, ys_prev2 = lax.rem(i, 3), lax.rem(i + 2, 3), lax.rem(i + 1, 3)
    rows = xbuf.shape[1] // ROW_SUB

    def gather_row(index_ref, s, r):
        _tile_copy(f_hbm, xbuf.at[s], in_sems.at[s], index_ref[0, 0, r], r).start()

    def scatter_row(r):
        _tile_copy(ybuf.at[ys_prev], y_hbm, out_sems.at[ys_prev], r, dst_prev_ref[0, 0, r]).start()

    def issue_loop(fn):
        def body(r, carry):
            fn(r)
            return carry
        lax.fori_loop(0, rows, body, 0, unroll=DMA_UNROLL)

    def compute(issue):
        xb = jnp.concatenate([xbuf[xs, pl.ds(j, rows, stride=ROW_SUB), :] for j in range(ROW_SUB)],
                             axis=-1).astype(BF16)
        dff = wg_ref.shape[1]
        acts = []
        for q in range(EXPERT_COL_CHUNKS):
            cs = slice(q * dff // EXPERT_COL_CHUNKS, (q + 1) * dff // EXPERT_COL_CHUNKS)
            glu = jnp.minimum(_dot(xb, wg_ref[:, cs]) + bg_ref[:, cs], SWIGLU_LIMIT)
            lin = jnp.clip(_dot(xb, wu_ref[:, cs]) + bu_ref[:, cs], -SWIGLU_LIMIT, SWIGLU_LIMIT)
            acts.append((glu * jax.nn.sigmoid(SWIGLU_ALPHA * glu) * (lin + 1.0)).astype(BF16))
            issue(q)
        y = _dot(jnp.concatenate(acts, axis=-1), wd_ref[...]) + bd_ref[...]
        for j in range(ROW_SUB):
            ybuf[ys, pl.ds(j, rows, stride=ROW_SUB), :] = y[:, j * LANES:(j + 1) * LANES]

    @pl.when(i >= 3)
    def _():
        _wait_tiles(y_hbm, ybuf.at[ys], out_sems.at[ys])

    @pl.when(i == n_blocks + 1)
    def _():
        _wait_tiles(y_hbm, ybuf.at[ys_prev2], out_sems.at[ys_prev2])

    steady = (i >= 1) & (i + 1 < n_used)

    @pl.when(steady)
    def _():
        _wait_tiles(f_hbm, xbuf.at[xs], in_sems.at[xs])
        per = rows // EXPERT_COL_CHUNKS

        def issue(q):
            for r in range(q * per, (q + 1) * per):
                gather_row(tok_next_ref, xs_next, r)
                scatter_row(r)
        compute(issue)

    @pl.when(jnp.logical_not(steady))
    def _():
        @pl.when((i == 0) & (n_used > 0))
        def _():
            issue_loop(lambda r: gather_row(tok_ref, 0, r))

        @pl.when(i + 1 < n_used)
        def _():
            issue_loop(lambda r: gather_row(tok_next_ref, xs_next, r))

        @pl.when(i < n_used)
        def _():
            _wait_tiles(f_hbm, xbuf.at[xs], in_sems.at[xs])
            compute(lambda q: None)

        @pl.when((i >= n_used) & (i < n_blocks))
        def _():
            ybuf[ys] = jnp.zeros(ybuf.shape[1:], F32)

        @pl.when((i >= 1) & (i <= n_blocks))
        def _():
            issue_loop(scatter_row)


def _experts(block_e, n_used, row_tok, row_slot, f, wg, bg, wu, bu, wd, bd):
    rows = MOE_ROWS
    n_blocks = row_tok.shape[0] // rows
    smem_blk = lambda fn: pl.BlockSpec((1, 1, rows), fn, memory_space=pltpu.SMEM)
    at = lambda off: (lambda i, be, nu: (jnp.clip(i + off, 0, n_blocks - 1), 0, 0))
    wspec = lambda w: pl.BlockSpec((None,) + w.shape[1:], lambda i, be, nu: (be[jnp.minimum(i, n_blocks - 1)], 0, 0))
    tok3 = row_tok.reshape(n_blocks, 1, rows)
    return pl.pallas_call(
        _experts_kernel,
        grid_spec=pltpu.PrefetchScalarGridSpec(
            num_scalar_prefetch=2,
            grid=(n_blocks + 2,),
            in_specs=[smem_blk(at(0)), smem_blk(at(1)), smem_blk(at(-1)),
                      pl.BlockSpec(memory_space=pl.ANY),
                      wspec(wg), wspec(bg), wspec(wu), wspec(bu), wspec(wd), wspec(bd)],
            out_specs=pl.BlockSpec(memory_space=pl.ANY),
            scratch_shapes=[pltpu.VMEM((2, rows * ROW_SUB, LANES), F32), pltpu.VMEM((3, rows * ROW_SUB, LANES), F32),
                            pltpu.SemaphoreType.DMA((2,)), pltpu.SemaphoreType.DMA((3,))],
        ),
        out_shape=jax.ShapeDtypeStruct((n_blocks * rows * ROW_SUB, LANES), F32),
        compiler_params=_params("arbitrary"),
        name="experts",
    )(block_e, n_used, tok3, tok3, row_slot.reshape(n_blocks, 1, rows), f, wg, bg, wu, bu, wd, bd)


def _combine_kernel(y_ref, h_ref, topg_ref, g_ref, o_ref, *, final):
    t, d = h_ref.shape
    gate = topg_ref[...]
    pieces = []
    for j in range(ROW_SUB):
        acc = h_ref[:, j * LANES:(j + 1) * LANES]
        for kk in range(TOP_K):
            acc = acc + gate[:, kk:kk + 1] * y_ref[pl.ds(kk * ROW_SUB + j, t, stride=TOP_K * ROW_SUB), :]
        pieces.append(acc)
    if final:
        ssq = pieces[0] * pieces[0]
        for piece in pieces[1:]:
            ssq = ssq + piece * piece
        scale = lax.rsqrt(jnp.sum(ssq, axis=-1, keepdims=True) * (1.0 / d) + EPS)
        pieces = [piece * scale * g_ref[:, j * LANES:(j + 1) * LANES] for j, piece in enumerate(pieces)]
    for j, piece in enumerate(pieces):
        o_ref[:, j * LANES:(j + 1) * LANES] = piece


def _combine(y, h, topg, g, final):
    m, d = h.shape
    t = COMBINE_TILE
    row = lambda n: pl.BlockSpec((t, n), lambda i: (i, 0))
    return pl.pallas_call(
        functools.partial(_combine_kernel, final=final),
        grid=(m // t,),
        in_specs=[pl.BlockSpec((t * TOP_K * ROW_SUB, LANES), lambda i: (i, 0)), row(d), row(LANES),
                  pl.BlockSpec(g.shape, lambda i: (0, 0))],
        out_specs=row(d),
        out_shape=jax.ShapeDtypeStruct((m, d), F32),
        compiler_params=_params("parallel"),
        name="combine",
    )(y, h, topg, g)


def _pad_lanes(w, axis):
    pad = [(0, 0)] * w.ndim
    pad[axis] = (0, LANES - w.shape[axis])
    return jnp.pad(w, pad)


def kernel(x, mem, positions, norm1_g, w_in, w_gla_a2, b_gla_a2, gla_norm_g, w_branch_gla, w_branch_dil, w_branch_gate, b_branch_gate, w_mix_out, norm2_g, mem_norm_g, w_xq, w_xkv, w_xo, norm3_g, w_router, b_router, w_gate, b_gate, w_up, b_up, w_down, b_down, final_norm_g):
    batch, seq, d = x.shape
    m = batch * seq
    h = x.reshape(m, d)
    pos2 = positions.reshape(m, 1)
    row = lambda v: v.reshape(1, -1)
    half = DIL_HEAD_DIM // 2
    inv_freq = ROPE_THETA ** (-jnp.arange(half, dtype=F32) / half)
    invf = jnp.concatenate([inv_freq, inv_freq]).reshape(1, DIL_HEAD_DIM)
    off_alr = 2 * GLA_QK + 2 * GLA_V
    off_q = off_alr + GLA_GATE_RANK

    for l in range(w_in.shape[0]):
        w_l = w_in[l]
        g1 = row(norm1_g[l])
        q_g, k_g, v_g, r_g, log_a = _gla_proj(
            h, g1, w_l[:, :off_alr].astype(BF16), _pad_lanes(w_l[:, off_alr:off_q], 1).astype(BF16),
            _pad_lanes(w_gla_a2[l], 0).astype(BF16), row(b_gla_a2[l]))
        dil = _dil_proj(h, pos2, g1, invf, w_l[:, off_q:off_q + DIL_W].astype(BF16),
                        w_l[:, off_q + DIL_W:off_q + 2 * DIL_W].astype(BF16),
                        w_l[:, off_q + 2 * DIL_W:].astype(BF16), batch, seq)
        o_g = _gla(q_g, k_g, v_g, r_g, log_a, row(gla_norm_g[l]), batch, seq)
        ods, lses = [], []
        for g in range(DIL_GROUPS):
            o_i, lse_i = _dil_attn(dil[g], dil[DIL_GROUPS + g], dil[2 * DIL_GROUPS + g])
            ods.append(o_i)
            lses.append(lse_i)
        kv = _mem_kv(mem, row(mem_norm_g[l]), w_xkv[l].astype(BF16))
        h2, f, topi, topg, rank, cnt = _post_mixer(
            h, o_g, ods, lses, kv, seq, g1, w_branch_gate[l].astype(BF16), row(b_branch_gate[l]),
            w_branch_gla[l].astype(BF16), w_branch_dil[l].astype(BF16), w_mix_out[l].astype(BF16),
            row(norm2_g[l]), w_xq[l].astype(BF16), w_xo[l].astype(BF16), row(norm3_g[l]),
            _pad_lanes(w_router[l], 1), _pad_lanes(row(b_router[l]), 1))

        rows = MOE_ROWS
        counts = cnt[0, :N_EXPERTS].astype(I32)
        padded = ((counts + rows - 1) // rows) * rows
        pend = jnp.cumsum(padded)
        pstart = pend - padded
        n_assign = m * TOP_K
        n_blocks = -(-n_assign // rows) + N_EXPERTS
        dest = pstart[topi[:, :TOP_K]] + rank[:, :TOP_K]
        slot_of_row = jnp.full((n_blocks * rows,), -1, I32).at[dest.reshape(-1)].set(
            jnp.arange(n_assign, dtype=I32), unique_indices=True, mode="promise_in_bounds")
        is_pad = slot_of_row < 0
        row_slot = jnp.where(is_pad, n_assign - 1 + jnp.cumsum(is_pad.astype(I32)), slot_of_row)
        row_tok = jnp.where(is_pad, 0, slot_of_row // TOP_K)
        block_start = jnp.arange(n_blocks, dtype=I32) * rows
        block_e = jnp.minimum(jnp.sum((block_start[:, None] >= pend[None, :]).astype(I32), axis=1), N_EXPERTS - 1)
        n_used = (pend[-1:] // rows).astype(I32)
        y = _experts(block_e, n_used, row_tok, row_slot, f, w_gate[l].astype(BF16), b_gate[l][:, None, :],
                     w_up[l].astype(BF16), b_up[l][:, None, :], w_down[l].astype(BF16), b_down[l][:, None, :])
        h = _combine(y, h2, topg, row(final_norm_g), final=l == w_in.shape[0] - 1)
    return h.reshape(batch, seq, d)
```

```python
import functools

import numpy as np
import jax
import jax.numpy as jnp
from jax import lax
from jax.experimental import pallas as pl
from jax.experimental.pallas import tpu as pltpu

F32 = jnp.float32
BF16 = jnp.bfloat16
I32 = jnp.int32

EPS = 1e-5
ROPE_THETA = 10000.0
GLA_HEADS = 4
GLA_DK = 64
GLA_DV = 128
GLA_GATE_RANK = 16
GLA_GATE_TEMP = 16.0
GLA_CHUNK = 64
GLA_QK = GLA_HEADS * GLA_DK
GLA_V = GLA_HEADS * GLA_DV
DIL_PAIRS = ((128, 1), (512, 4), (2048, 16))
DIL_GROUPS = 3
DIL_HEADS = 4
DIL_HEAD_DIM = 128
DIL_BLOCK = 128
DIL_GW = DIL_HEADS * DIL_HEAD_DIM
DIL_W = DIL_GROUPS * DIL_GW
XATTN_HEADS = 4
XATTN_HEAD_DIM = 128
N_EXPERTS = 32
TOP_K = 4
SWIGLU_ALPHA = 1.702
SWIGLU_LIMIT = 7.0

LANES = 128
NEG_BIG = -1e30
VMEM_LIMIT = 56 * 1024 * 1024

TOKEN_TILE = 512
POST_TILE = 256
GLA_TILE = 512
MOE_ROWS = 256
COMBINE_TILE = 256


def _params(*sem):
    return pltpu.CompilerParams(dimension_semantics=sem, vmem_limit_bytes=VMEM_LIMIT)


def _resident(shape):
    nd = len(shape)
    return pl.BlockSpec(shape, lambda *_: (0,) * nd, pipeline_mode=pl.Buffered(1))


def _rms(x, g):
    return x * lax.rsqrt(jnp.mean(x * x, axis=-1, keepdims=True) + EPS) * g


def _dot(a, b):
    return jnp.dot(a, b, preferred_element_type=F32)


def _dot_nt(a, b):
    return lax.dot_general(a, b, (((1,), (1,)), ((), ())), preferred_element_type=F32)


def _gla_proj_kernel(x_ref, g_ref, w_ref, walr_ref, wa2_ref, ba2_ref,
                     q_ref, k_ref, v_ref, r_ref, la_ref):
    a = _rms(x_ref[...], g_ref[...]).astype(BF16)
    y = _dot(a, w_ref[...])
    q_ref[...] = (y[:, :GLA_QK] * (GLA_DK ** -0.5)).astype(BF16)
    k_ref[...] = y[:, GLA_QK:2 * GLA_QK].astype(BF16)
    v_ref[...] = y[:, 2 * GLA_QK:2 * GLA_QK + GLA_V].astype(BF16)
    r = y[:, 2 * GLA_QK + GLA_V:]
    r_ref[...] = (r * jax.nn.sigmoid(r)).astype(BF16)
    a_lr = _dot(a, walr_ref[...])
    logit = _dot(a_lr.astype(BF16), wa2_ref[...]) + ba2_ref[...]
    log_sig = jnp.minimum(logit, 0.0) - jnp.log1p(jnp.exp(-jnp.abs(logit)))
    la_ref[...] = log_sig * (1.0 / GLA_GATE_TEMP)


def _gla_proj(x2, g, w, walr, wa2, ba2):
    m, d = x2.shape
    t = TOKEN_TILE
    row = lambda n: pl.BlockSpec((t, n), lambda i: (i, 0))
    return pl.pallas_call(
        _gla_proj_kernel,
        grid=(m // t,),
        in_specs=[row(d), _resident(g.shape), _resident(w.shape), _resident(walr.shape),
                  _resident(wa2.shape), _resident(ba2.shape)],
        out_specs=[row(GLA_QK), row(GLA_QK), row(GLA_V), row(GLA_V), row(GLA_QK)],
        out_shape=[jax.ShapeDtypeStruct((m, GLA_QK), BF16), jax.ShapeDtypeStruct((m, GLA_QK), BF16),
                   jax.ShapeDtypeStruct((m, GLA_V), BF16), jax.ShapeDtypeStruct((m, GLA_V), BF16),
                   jax.ShapeDtypeStruct((m, GLA_QK), F32)],
        compiler_params=_params("parallel"),
        name="gla_proj",
    )(x2, g, w, walr, wa2, ba2)


def _dil_proj_kernel(x_ref, pos_ref, g_ref, invf_ref, wq_ref, wk_ref, wv_ref, *refs):
    q_refs, k_refs, v_refs = refs[0:3], refs[3:6], refs[6:9]
    stage = refs[9:]
    a = _rms(x_ref[...], g_ref[...]).astype(BF16)
    t = a.shape[0]
    ang = pos_ref[...].astype(F32) * invf_ref[...]
    cos, sin = jnp.cos(ang), jnp.sin(ang)
    lane = lax.broadcasted_iota(I32, ang.shape, 1)
    sin_signed = jnp.where(lane < DIL_HEAD_DIM // 2, -sin, sin)

    def rope(v):
        return v * cos + pltpu.roll(v, DIL_HEAD_DIM // 2, 1) * sin_signed

    ys = (_dot(a, wq_ref[...]), _dot(a, wk_ref[...]), _dot(a, wv_ref[...]))
    post = (lambda v: rope(v) * (DIL_HEAD_DIM ** -0.5), rope, lambda v: v)
    for ti, out_refs in enumerate((q_refs, k_refs, v_refs)):
        for g, (_, dil) in enumerate(DIL_PAIRS):
            for h in range(DIL_HEADS):
                c0 = g * DIL_GW + h * DIL_HEAD_DIM
                hs = slice(h * DIL_HEAD_DIM, (h + 1) * DIL_HEAD_DIM)
                val = post[ti](ys[ti][:, c0:c0 + DIL_HEAD_DIM])
                if dil == 1:
                    out_refs[g][0, :, hs] = val.astype(BF16)
                else:
                    buf = stage[ti * (DIL_GROUPS - 1) + g - 1]
                    buf[h] = val
                    for p in range(dil):
                        out_refs[g][p, :, hs] = buf[h, pl.ds(p, t // dil, stride=dil), :].astype(BF16)


def _dil_proj(x2, pos2, g, invf, wq, wk, wv, batch, seq):
    m, d = x2.shape
    t = TOKEN_TILE
    nt = seq // t
    row = lambda n: pl.BlockSpec((t, n), lambda i: (i, 0))
    out_specs, out_shape = [], []
    for _ in range(3):
        for _, dil in DIL_PAIRS:
            out_specs.append(pl.BlockSpec((None, dil, t // dil, DIL_GW), lambda i: (i // nt, 0, i % nt, 0)))
            out_shape.append(jax.ShapeDtypeStruct((batch, dil, seq // dil, DIL_GW), BF16))
    return pl.pallas_call(
        _dil_proj_kernel,
        grid=(m // t,),
        in_specs=[row(d), row(1), _resident(g.shape), _resident(invf.shape),
                  _resident(wq.shape), _resident(wk.shape), _resident(wv.shape)],
        out_specs=out_specs,
        out_shape=out_shape,
        scratch_shapes=[pltpu.VMEM((DIL_HEADS, t, DIL_HEAD_DIM), F32)] * (3 * (DIL_GROUPS - 1)),
        compiler_params=_params("parallel"),
        name="dil_proj",
    )(x2, pos2, g, invf, wq, wk, wv)


GLA_LEVELS = (32, 16, 8, 4, 2, 1)
GLA_ONE_FACTOR_MIN = -60.0


def _gla_level_masks():
    c = GLA_CHUNK
    t = np.arange(c)[:, None]
    r = np.arange(c)[None, :]
    blocks = []
    for h in GLA_LEVELS:
        u = t % (2 * h)
        m = t - u + h - 1
        blocks.append(((u >= h) & (r > m) & (r <= t)).astype(np.float32))
        blocks.append(((u < h) & (r > t) & (r <= m)).astype(np.float32))
    return np.concatenate(blocks, axis=0)


def _gla_kernel(q_ref, k_ref, v_ref, r_ref, la_ref, gn_ref, tri_ref, lvl_ref, o_ref, st_ref, bc_ref):
    @pl.when(pl.program_id(1) == 0)
    def _():
        st_ref[...] = jnp.zeros_like(st_ref)

    c = GLA_CHUNK
    n_chunks = q_ref.shape[0] // c
    row_i = lax.broadcasted_iota(I32, (c, c), 0)
    col_i = lax.broadcasted_iota(I32, (c, c), 1)
    gn = gn_ref[...]

    tri = tri_ref[...]
    min_last = None
    for ci in range(n_chunks):
        rows = slice(ci * c, (ci + 1) * c)
        bcum = jnp.dot(tri, la_ref[rows, :], precision=lax.Precision.HIGHEST,
                       preferred_element_type=F32)
        bc_ref[rows, :] = bcum
        last = bcum[c - 1:c, :]
        min_last = last if min_last is None else jnp.minimum(min_last, last)
    one_factor_ok = jnp.min(min_last) >= GLA_ONE_FACTOR_MIN

    def finish_chunk(rows, q_dec, k_dec, e_last, v, r, scores_of_head):
        for h in range(GLA_HEADS):
            ks = slice(h * GLA_DK, (h + 1) * GLA_DK)
            vs = slice(h * GLA_DV, (h + 1) * GLA_DV)
            vh = v[:, vs]
            st = st_ref[h]
            o = _dot(scores_of_head(h, ks).astype(BF16), vh) + _dot_nt(q_dec[:, ks], st.astype(BF16))
            v_t = vh.astype(F32).T.astype(BF16)
            st_ref[h] = st * e_last[:, ks] + _dot(v_t, k_dec[:, ks])
            o = o * lax.rsqrt(jnp.mean(o * o, axis=-1, keepdims=True) + EPS) * gn[:, vs]
            o_ref[rows, vs] = (o * r[:, vs]).astype(BF16)

    def load_chunk(rows):
        q = q_ref[rows, :].astype(F32)
        k = k_ref[rows, :].astype(F32)
        bcum = bc_ref[rows, :]
        blast = bcum[c - 1:c, :]
        q_dec = (q * jnp.exp(bcum)).astype(BF16)
        k_dec = (k * jnp.exp(blast - bcum)).astype(BF16)
        return q, k, bcum, q_dec, k_dec, jnp.exp(blast), v_ref[rows, :], r_ref[rows, :].astype(F32)

    @pl.when(one_factor_ok)
    def _():
        for ci in range(n_chunks):
            rows = slice(ci * c, (ci + 1) * c)
            q, k, bcum, q_dec, k_dec, e_last, v, r = load_chunk(rows)
            k_inv = (k * jnp.exp(-bcum)).astype(BF16)
            finish_chunk(rows, q_dec, k_dec, e_last, v, r,
                         lambda h, ks: jnp.where(row_i >= col_i, _dot_nt(q_dec[:, ks], k_inv[:, ks]), 0.0))

    @pl.when(jnp.logical_not(one_factor_ok))
    def _():
        def chunk(ci, carry):
            rows = pl.ds(pl.multiple_of(ci * c, c), c)
            q, k, bcum, q_dec, k_dec, e_last, v, r = load_chunk(rows)
            expo = jnp.dot(lvl_ref[...], la_ref[rows, :], precision=lax.Precision.HIGHEST,
                           preferred_element_type=F32)
            q_lv = [(q * jnp.exp(expo[2 * j * c:(2 * j + 1) * c, :])).astype(BF16) for j in range(len(GLA_LEVELS))]
            k_lv = [(k * jnp.exp(expo[(2 * j + 1) * c:(2 * j + 2) * c, :])).astype(BF16) for j in range(len(GLA_LEVELS))]
            q_bf, k_bf = q.astype(BF16), k.astype(BF16)

            def scores_of_head(h, ks):
                sc = jnp.where(row_i == col_i, _dot_nt(q_bf[:, ks], k_bf[:, ks]), 0.0)
                differ = row_i ^ col_i
                for j, half in enumerate(GLA_LEVELS):
                    valid = (differ >= half) & (differ < 2 * half) & ((row_i & half) != 0)
                    sc = sc + jnp.where(valid, _dot_nt(q_lv[j][:, ks], k_lv[j][:, ks]), 0.0)
                return sc
            finish_chunk(rows, q_dec, k_dec, e_last, v, r, scores_of_head)
            return carry
        lax.fori_loop(0, n_chunks, chunk, 0)


def _gla(q, k, v, r, la, gn, batch, seq):
    t = GLA_TILE
    nt = seq // t
    row = lambda n: pl.BlockSpec((t, n), lambda b, j: (b * nt + j, 0))
    const = lambda a: pl.BlockSpec(a.shape, lambda b, j: (0, 0))
    tri = jnp.tril(jnp.ones((GLA_CHUNK, GLA_CHUNK), F32))
    lvl = jnp.asarray(_gla_level_masks())
    return pl.pallas_call(
        _gla_kernel,
        grid=(batch, nt),
        in_specs=[row(GLA_QK), row(GLA_QK), row(GLA_V), row(GLA_V), row(GLA_QK), const(gn), const(tri), const(lvl)],
        out_specs=row(GLA_V),
        out_shape=jax.ShapeDtypeStruct((batch * seq, GLA_V), BF16),
        scratch_shapes=[pltpu.VMEM((GLA_HEADS, GLA_DV, GLA_DK), F32), pltpu.VMEM((t, GLA_QK), F32)],
        compiler_params=_params("parallel", "arbitrary"),
        name="gla",
    )(q, k, v, r, la, gn, tri, lvl)


def _dil_attn_kernel(q_ref, k_ref, v_ref, o_ref, lse_ref, kp_ref, vp_ref):
    n = pl.program_id(2)

    @pl.when(n == 0)
    def _():
        kp_ref[...] = jnp.zeros_like(kp_ref)
        vp_ref[...] = jnp.zeros_like(vp_ref)

    qb = DIL_BLOCK
    row = lax.broadcasted_iota(I32, (qb, qb), 0)
    col = lax.broadcasted_iota(I32, (qb, qb), 1)
    lower = col <= row
    prev_bias = jnp.where(n > 0, 0.0, NEG_BIG)
    lane = lax.broadcasted_iota(I32, (qb, LANES), 1)
    lse_all = jnp.zeros((qb, LANES), F32)
    for h in range(DIL_HEADS):
        hs = slice(h * DIL_HEAD_DIM, (h + 1) * DIL_HEAD_DIM)
        q = q_ref[:, hs]
        s_cur = _dot_nt(q, k_ref[:, hs])
        s_prev = _dot_nt(q, kp_ref[:, hs])
        s = jnp.where(lower, s_cur, s_prev + prev_bias)
        s_far = jnp.sum(jnp.where(row == col, s_prev, 0.0), axis=-1, keepdims=True) + prev_bias
        m = jnp.maximum(jnp.max(s, axis=-1, keepdims=True), s_far)
        p = jnp.exp(s - m)
        p_far = jnp.exp(s_far - m)
        l = jnp.sum(p, axis=-1, keepdims=True) + p_far
        vp = vp_ref[:, hs]
        o = (_dot(jnp.where(lower, p, 0.0).astype(BF16), v_ref[:, hs])
             + _dot(jnp.where(lower, 0.0, p).astype(BF16), vp) + p_far * vp.astype(F32))
        o_ref[:, hs] = (o / l).astype(BF16)
        lse_all = jnp.where(lane == h, m + jnp.log(l), lse_all)
    lse_ref[...] = lse_all
    kp_ref[...] = k_ref[...]
    vp_ref[...] = v_ref[...]


def _dil_attn(q, k, v):
    batch, dilation, length, _ = q.shape
    blk = lambda n: pl.BlockSpec((None, None, DIL_BLOCK, n), lambda b, p, i: (b, p, i, 0))
    return pl.pallas_call(
        _dil_attn_kernel,
        grid=(batch, dilation, length // DIL_BLOCK),
        in_specs=[blk(DIL_GW)] * 3,
        out_specs=[blk(DIL_GW), blk(LANES)],
        out_shape=[jax.ShapeDtypeStruct(q.shape, BF16),
                   jax.ShapeDtypeStruct((batch, dilation, length, LANES), F32)],
        scratch_shapes=[pltpu.VMEM((DIL_BLOCK, DIL_GW), BF16), pltpu.VMEM((DIL_BLOCK, DIL_GW), BF16)],
        compiler_params=_params("parallel", "parallel", "arbitrary"),
        name=f"dil_attn_d{dilation}",
    )(q, k, v)


def _mem_kv_kernel(mem_ref, g_ref, w_ref, kv_ref):
    kv_ref[...] = _dot(_rms(mem_ref[...], g_ref[...]).astype(BF16), w_ref[...]).astype(BF16)


def _mem_kv(mem, g, w):
    b, ml, d = mem.shape
    n = w.shape[1]
    return pl.pallas_call(
        _mem_kv_kernel,
        grid=(b,),
        in_specs=[pl.BlockSpec((None, ml, d), lambda i: (i, 0, 0)), _resident(g.shape), _resident(w.shape)],
        out_specs=pl.BlockSpec((None, ml, n), lambda i: (i, 0, 0)),
        out_shape=jax.ShapeDtypeStruct((b, ml, n), BF16),
        compiler_params=_params("parallel"),
        name="mem_kv",
    )(mem, g, w)


def _post_mixer_kernel(x_ref, og_ref, od0_ref, od1_ref, od2_ref, l0_ref, l1_ref, l2_ref, kv_ref,
                       g1_ref, wgate_ref, bgate_ref, wbg_ref, wbd_ref, wmix_ref,
                       g2_ref, wxq_ref, wxo_ref, g3_ref, wr_ref, br_ref,
                       h_ref, f_ref, pos_ref, topg_ref, cnt_ref,
                       o1_buf, o2_buf, l1_buf, l2_buf):
    x = x_ref[...]
    t, d = x.shape
    a = _rms(x, g1_ref[...]).astype(BF16)

    for od_ref, l_ref, o_buf, l_buf in ((od1_ref, l1_ref, o1_buf, l1_buf), (od2_ref, l2_ref, o2_buf, l2_buf)):
        dil = od_ref.shape[0]
        for p in range(dil):
            rows = pl.ds(p, t // dil, stride=dil)
            l_buf[rows, :] = l_ref[p]
            for h in range(DIL_HEADS):
                o_buf[h, rows, :] = od_ref[p, :, h * DIL_HEAD_DIM:(h + 1) * DIL_HEAD_DIM].astype(F32)

    l0, l1, l2 = l0_ref[0], l1_buf[...], l2_buf[...]
    lm = jnp.maximum(jnp.maximum(l0, l1), l2)
    e0, e1, e2 = jnp.exp(l0 - lm), jnp.exp(l1 - lm), jnp.exp(l2 - lm)
    inv = 1.0 / (e0 + e1 + e2)
    w0, w1, w2 = e0 * inv, e1 * inv, e2 * inv
    od_heads = []
    for h in range(DIL_HEADS):
        hs = slice(h * DIL_HEAD_DIM, (h + 1) * DIL_HEAD_DIM)
        od_heads.append(w0[:, h:h + 1] * od0_ref[0, :, hs].astype(F32)
                        + w1[:, h:h + 1] * o1_buf[h] + w2[:, h:h + 1] * o2_buf[h])
    o_d = jnp.concatenate(od_heads, axis=-1).astype(BF16)

    gates = jax.nn.sigmoid(_dot(a, wgate_ref[...]) + bgate_ref[...])
    merged = gates[:, :d] * _dot(og_ref[...], wbg_ref[...]) + gates[:, d:] * _dot(o_d, wbd_ref[...])
    h1 = x + _dot(merged.astype(BF16), wmix_ref[...])

    c = _rms(h1, g2_ref[...]).astype(BF16)
    qx = (_dot(c, wxq_ref[...]) * (XATTN_HEAD_DIM ** -0.5)).astype(BF16)
    xw = XATTN_HEADS * XATTN_HEAD_DIM
    heads = []
    for h in range(XATTN_HEADS):
        hs = slice(h * XATTN_HEAD_DIM, (h + 1) * XATTN_HEAD_DIM)
        vs = slice(xw + h * XATTN_HEAD_DIM, xw + (h + 1) * XATTN_HEAD_DIM)
        s = _dot_nt(qx[:, hs], kv_ref[:, hs])
        p = jnp.exp(s - jnp.max(s, axis=-1, keepdims=True))
        o = _dot(p.astype(BF16), kv_ref[:, vs]) / jnp.sum(p, axis=-1, keepdims=True)
        heads.append(o.astype(BF16))
    h2 = h1 + _dot(jnp.concatenate(heads, axis=-1), wxo_ref[...])
    h_ref[...] = h2

    f = _rms(h2, g3_ref[...])
    logits = jnp.dot(f, wr_ref[...], precision=lax.Precision.HIGHEST,
                     preferred_element_type=F32) + br_ref[...]
    lane = lax.broadcasted_iota(I32, (t, LANES), 1)
    work = jnp.where(lane < N_EXPERTS, logits, NEG_BIG)
    selected = jnp.zeros((t, LANES), F32)
    top_val, top_idx = [], []
    for _ in range(TOP_K):
        mx = jnp.max(work, axis=-1, keepdims=True)
        idx = jnp.min(jnp.where(work == mx, lane, LANES), axis=-1, keepdims=True)
        hit = lane == idx
        selected = jnp.where(hit, 1.0, selected)
        work = jnp.where(hit, NEG_BIG, work)
        top_val.append(mx)
        top_idx.append(idx)
    ex = [jnp.exp(v - top_val[0]) for v in top_val]
    inv_den = 1.0 / (ex[0] + ex[1] + ex[2] + ex[3])

    sel16 = selected.astype(BF16)
    earlier = lax.broadcasted_iota(I32, (t, t), 1) < lax.broadcasted_iota(I32, (t, t), 0)
    before = _dot(jnp.where(earlier, 1.0, 0.0).astype(BF16), sel16)
    count = jnp.sum(selected, axis=0, keepdims=True)
    lower_expert = lax.broadcasted_iota(I32, (LANES, LANES), 0) < lax.broadcasted_iota(I32, (LANES, LANES), 1)
    offset = _dot(jnp.broadcast_to(count, (8, LANES)).astype(BF16),
                  jnp.where(lower_expert, 1.0, 0.0).astype(BF16))[0:1, :]
    where_to = before + offset
    pos = jnp.zeros((t, LANES), F32)
    topg = jnp.zeros((t, LANES), F32)
    for kk in range(TOP_K):
        pk = jnp.sum(jnp.where(lane == top_idx[kk], where_to, 0.0), axis=-1, keepdims=True)
        pos = jnp.where(lane == kk, pk, pos)
        topg = jnp.where(lane == kk, ex[kk] * inv_den, topg)
    pos_ref[...] = pos.astype(I32)
    topg_ref[...] = topg
    cnt_ref[...] = jnp.broadcast_to(count, cnt_ref.shape)

    pos_rows = pos.T
    slot_i = lax.broadcasted_iota(I32, (t * TOP_K, t), 0).astype(F32)
    perm = jnp.zeros((t * TOP_K, t), F32)
    for kk in range(TOP_K):
        perm = jnp.where(slot_i == pos_rows[kk:kk + 1, :], 1.0, perm)
    f_sorted = _dot(perm.astype(BF16), f.astype(BF16))
    for j in range(d // LANES):
        f_ref[pl.ds(j, t * TOP_K, stride=d // LANES), :] = f_sorted[:, j * LANES:(j + 1) * LANES]


def _post_mixer(x2, og, ods, lses, kv, seq, g1, wgate, bgate, wbg, wbd, wmix, g2, wxq, wxo, g3, wr, br):
    m, d = x2.shape
    t = POST_TILE
    nt = seq // t
    sub = d // LANES
    row = lambda n: pl.BlockSpec((t, n), lambda i: (i, 0))
    phase = lambda a: pl.BlockSpec((None, a.shape[1], t // a.shape[1], a.shape[3]),
                                   lambda i: (i // nt, 0, i % nt, 0))
    kv_spec = pl.BlockSpec((None,) + kv.shape[1:], lambda i: (i // nt, 0, 0))
    weights = [g1, wgate, bgate, wbg, wbd, wmix, g2, wxq, wxo, g3, wr, br]
    return pl.pallas_call(
        _post_mixer_kernel,
        grid=(m // t,),
        in_specs=[row(d), row(GLA_V)] + [phase(a) for a in ods] + [phase(a) for a in lses] + [kv_spec]
                 + [_resident(w.shape) for w in weights],
        out_specs=[row(d), pl.BlockSpec((t * TOP_K * sub, LANES), lambda i: (i, 0)), row(LANES), row(LANES),
                   pl.BlockSpec((8, LANES), lambda i: (i, 0))],
        out_shape=[jax.ShapeDtypeStruct((m, d), F32), jax.ShapeDtypeStruct((m * TOP_K * sub, LANES), F32),
                   jax.ShapeDtypeStruct((m, LANES), I32), jax.ShapeDtypeStruct((m, LANES), F32),
                   jax.ShapeDtypeStruct((m // t * 8, LANES), F32)],
        scratch_shapes=[pltpu.VMEM((DIL_HEADS, t, DIL_HEAD_DIM), F32), pltpu.VMEM((DIL_HEADS, t, DIL_HEAD_DIM), F32),
                        pltpu.VMEM((t, LANES), F32), pltpu.VMEM((t, LANES), F32)],
        compiler_params=_params("parallel"),
        name="post_mixer",
    )(x2, og, *ods, *lses, kv, *weights)


ROW_SUB = 8
RUN_BITS = tuple(1 << b for b in range(8, -1, -1))


def _wait_tiles(hbm, vmem, sem):
    pltpu.make_async_copy(hbm.at[pl.ds(0, vmem.shape[0]), :], vmem, sem).wait()


def _experts_kernel(be_ref, nu_ref, r0_ref, nv_ref, jlo_ref, jhi_ref, cum_ref, off_ref,
                    f_hbm, wg_ref, bg_ref, wu_ref, bu_ref, wd_ref, bd_ref, y_hbm,
                    xbuf, ybuf, in_sems, out_sems, *, n_tiles, tile_rows):
    i = pl.program_id(0)
    n_used = nu_ref[0]
    rows = xbuf.shape[1] // ROW_SUB
    spare_row = n_tiles * tile_rows

    def copy_rows(hbm_row, buf_row, count, start_piece):
        done = jnp.int32(0)
        for bit in RUN_BITS:
            if bit > rows:
                continue
            take = (count & bit) != 0

            @pl.when(take)
            def _():
                start_piece(hbm_row + done, buf_row + done, bit)
            done = done + jnp.where(take, bit, 0)

    def for_runs(b, start_piece, filler_row):
        e, r0, nv = be_ref[b], r0_ref[b], nv_ref[b]

        def body(j, carry):
            lo, hi = cum_ref[e * (n_tiles + 1) + j], cum_ref[e * (n_tiles + 1) + j + 1]
            first = jnp.maximum(lo, r0)
            count = jnp.maximum(jnp.minimum(hi, r0 + nv) - first, 0)
            copy_rows(j * tile_rows + off_ref[j * N_EXPERTS + e] + first - lo, first - r0, count, start_piece)
            return carry
        lax.fori_loop(jlo_ref[b], jhi_ref[b], body, 0)
        copy_rows(filler_row, nv, rows - nv, start_piece)

    def gather(b, s):
        def piece(hbm_row, buf_row, n):
            pltpu.make_async_copy(f_hbm.at[pl.ds(hbm_row * ROW_SUB, n * ROW_SUB), :],
                                  xbuf.at[s, pl.ds(buf_row * ROW_SUB, n * ROW_SUB), :], in_sems.at[s]).start()
        for_runs(b, piece, 0)

    def scatter(b, s):
        def piece(hbm_row, buf_row, n):
            pltpu.make_async_copy(ybuf.at[s, pl.ds(buf_row * ROW_SUB, n * ROW_SUB), :],
                                  y_hbm.at[pl.ds(hbm_row * ROW_SUB, n * ROW_SUB), :], out_sems.at[s]).start()
        for_runs(b, piece, spare_row + s * rows)

    xs, ys = lax.rem(i, 2), lax.rem(i, 3)

    @pl.when(i == 0)
    def _():
        ybuf[2] = jnp.zeros(ybuf.shape[1:], F32)
        spare = [pltpu.make_async_copy(ybuf.at[2], y_hbm.at[pl.ds((spare_row + s * rows) * ROW_SUB, rows * ROW_SUB), :],
                                       out_sems.at[2]) for s in range(3)]
        for cp in spare:
            cp.start()
        for cp in spare:
            cp.wait()

    @pl.when((i >= 3) & (i - 3 < n_used))
    def _():
        _wait_tiles(y_hbm, ybuf.at[ys], out_sems.at[ys])

    @pl.when((i == 0) & (n_used > 0))
    def _():
        gather(0, 0)

    @pl.when(i + 1 < n_used)
    def _():
        gather(i + 1, lax.rem(i + 1, 2))

    @pl.when((i >= 1) & (i - 1 < n_used))
    def _():
        scatter(i - 1, lax.rem(i + 2, 3))

    @pl.when(i < n_used)
    def _():
        _wait_tiles(f_hbm, xbuf.at[xs], in_sems.at[xs])
        xb = jnp.concatenate([xbuf[xs, pl.ds(j, rows, stride=ROW_SUB), :] for j in range(ROW_SUB)],
                             axis=-1).astype(BF16)
        glu = jnp.minimum(_dot(xb, wg_ref[...]) + bg_ref[...], SWIGLU_LIMIT)
        lin = jnp.clip(_dot(xb, wu_ref[...]) + bu_ref[...], -SWIGLU_LIMIT, SWIGLU_LIMIT)
        act = glu * jax.nn.sigmoid(SWIGLU_ALPHA * glu) * (lin + 1.0)
        y = _dot(act.astype(BF16), wd_ref[...]) + bd_ref[...]
        for j in range(ROW_SUB):
            ybuf[ys, pl.ds(j, rows, stride=ROW_SUB), :] = y[:, j * LANES:(j + 1) * LANES]


def _experts(block_e, n_used, block_r0, block_nv, block_jlo, block_jhi, cum_flat, off_flat,
             f, wg, bg, wu, bu, wd, bd, n_tiles, tile_rows):
    rows = MOE_ROWS
    n_blocks = block_e.shape[0]
    wspec = lambda w: pl.BlockSpec((None,) + w.shape[1:],
                                   lambda i, be, *_: (be[jnp.minimum(i, n_blocks - 1)], 0, 0))
    return pl.pallas_call(
        functools.partial(_experts_kernel, n_tiles=n_tiles, tile_rows=tile_rows),
        grid_spec=pltpu.PrefetchScalarGridSpec(
            num_scalar_prefetch=8,
            grid=(n_blocks + 3,),
            in_specs=[pl.BlockSpec(memory_space=pl.ANY),
                      wspec(wg), wspec(bg), wspec(wu), wspec(bu), wspec(wd), wspec(bd)],
            out_specs=pl.BlockSpec(memory_space=pl.ANY),
            scratch_shapes=[pltpu.VMEM((2, rows * ROW_SUB, LANES), F32), pltpu.VMEM((3, rows * ROW_SUB, LANES), F32),
                            pltpu.SemaphoreType.DMA((2,)), pltpu.SemaphoreType.DMA((3,))],
        ),
        out_shape=jax.ShapeDtypeStruct(((n_tiles * tile_rows + 3 * rows) * ROW_SUB, LANES), F32),
        compiler_params=_params("arbitrary"),
        name="experts",
    )(block_e, n_used, block_r0, block_nv, block_jlo, block_jhi, cum_flat, off_flat, f, wg, bg, wu, bu, wd, bd)


def _combine_kernel(y_ref, h_ref, pos_ref, topg_ref, g_ref, o_ref, *, final):
    t, d = h_ref.shape
    n_sorted = t * TOP_K
    pos = pos_ref[...]
    gate = topg_ref[...]
    slot_i = lax.broadcasted_iota(I32, (t, n_sorted), 1)
    w = jnp.zeros((t, n_sorted), F32)
    for kk in range(TOP_K):
        w = jnp.where(slot_i == pos[:, kk:kk + 1], gate[:, kk:kk + 1], w)
    y = jnp.concatenate([y_ref[pl.ds(j, n_sorted, stride=ROW_SUB), :] for j in range(ROW_SUB)], axis=-1)
    acc = h_ref[...] + _dot(w.astype(BF16), y.astype(BF16))
    o_ref[...] = _rms(acc, g_ref[...]) if final else acc


def _combine(y, h, pos, topg, g, final):
    m, d = h.shape
    t = POST_TILE
    row = lambda n: pl.BlockSpec((t, n), lambda i: (i, 0))
    return pl.pallas_call(
        functools.partial(_combine_kernel, final=final),
        grid=(m // t,),
        in_specs=[pl.BlockSpec((t * TOP_K * ROW_SUB, LANES), lambda i: (i, 0)), row(d), row(LANES), row(LANES),
                  pl.BlockSpec(g.shape, lambda i: (0, 0))],
        out_specs=row(d),
        out_shape=jax.ShapeDtypeStruct((m, d), F32),
        compiler_params=_params("parallel"),
        name="combine",
    )(y, h, pos, topg, g)


def _pad_lanes(w, axis):
    pad = [(0, 0)] * w.ndim
    pad[axis] = (0, LANES - w.shape[axis])
    return jnp.pad(w, pad)


def kernel(x, mem, positions, norm1_g, w_in, w_gla_a2, b_gla_a2, gla_norm_g, w_branch_gla, w_branch_dil, w_branch_gate, b_branch_gate, w_mix_out, norm2_g, mem_norm_g, w_xq, w_xkv, w_xo, norm3_g, w_router, b_router, w_gate, b_gate, w_up, b_up, w_down, b_down, final_norm_g):
    batch, seq, d = x.shape
    m = batch * seq
    h = x.reshape(m, d)
    pos2 = positions.reshape(m, 1)
    row = lambda v: v.reshape(1, -1)
    half = DIL_HEAD_DIM // 2
    inv_freq = ROPE_THETA ** (-jnp.arange(half, dtype=F32) / half)
    invf = jnp.concatenate([inv_freq, inv_freq]).reshape(1, DIL_HEAD_DIM)
    off_alr = 2 * GLA_QK + 2 * GLA_V
    off_q = off_alr + GLA_GATE_RANK

    for l in range(w_in.shape[0]):
        w_l = w_in[l]
        g1 = row(norm1_g[l])
        q_g, k_g, v_g, r_g, log_a = _gla_proj(
            h, g1, w_l[:, :off_alr].astype(BF16), _pad_lanes(w_l[:, off_alr:off_q], 1).astype(BF16),
            _pad_lanes(w_gla_a2[l], 0).astype(BF16), row(b_gla_a2[l]))
        dil = _dil_proj(h, pos2, g1, invf, w_l[:, off_q:off_q + DIL_W].astype(BF16),
                        w_l[:, off_q + DIL_W:off_q + 2 * DIL_W].astype(BF16),
                        w_l[:, off_q + 2 * DIL_W:].astype(BF16), batch, seq)
        o_g = _gla(q_g, k_g, v_g, r_g, log_a, row(gla_norm_g[l]), batch, seq)
        ods, lses = [], []
        for g in range(DIL_GROUPS):
            o_i, lse_i = _dil_attn(dil[g], dil[DIL_GROUPS + g], dil[2 * DIL_GROUPS + g])
            ods.append(o_i)
            lses.append(lse_i)
        kv = _mem_kv(mem, row(mem_norm_g[l]), w_xkv[l].astype(BF16))
        h2, f, pos, topg, cnt = _post_mixer(
            h, o_g, ods, lses, kv, seq, g1, w_branch_gate[l].astype(BF16), row(b_branch_gate[l]),
            w_branch_gla[l].astype(BF16), w_branch_dil[l].astype(BF16), w_mix_out[l].astype(BF16),
            row(norm2_g[l]), w_xq[l].astype(BF16), w_xo[l].astype(BF16), row(norm3_g[l]),
            _pad_lanes(w_router[l], 1), _pad_lanes(row(b_router[l]), 1))

        rows = MOE_ROWS
        n_tiles = m // POST_TILE
        tile_rows = POST_TILE * TOP_K
        count = cnt.reshape(n_tiles, 8, LANES)[:, 0, :N_EXPERTS].astype(I32)
        cum = jnp.concatenate([jnp.zeros((1, N_EXPERTS), I32), jnp.cumsum(count, axis=0)], axis=0).T
        off = jnp.cumsum(count, axis=1) - count
        total = cum[:, -1]
        padded = ((total + rows - 1) // rows) * rows
        pend = jnp.cumsum(padded)
        pstart = pend - padded
        n_blocks = -(-(m * TOP_K) // rows) + N_EXPERTS
        block_start = jnp.arange(n_blocks, dtype=I32) * rows
        block_e = jnp.minimum(jnp.sum((block_start[:, None] >= pend[None, :]).astype(I32), axis=1), N_EXPERTS - 1)
        n_used = (pend[-1:] // rows).astype(I32)
        block_r0 = block_start - pstart[block_e]
        block_nv = jnp.clip(total[block_e] - block_r0, 0, rows)
        cum_b = cum[block_e]
        block_jlo = jnp.sum((cum_b[:, 1:] <= block_r0[:, None]).astype(I32), axis=1)
        block_jhi = jnp.sum((cum_b[:, :-1] < (block_r0 + block_nv)[:, None]).astype(I32), axis=1)
        y = _experts(block_e, n_used, block_r0, block_nv, block_jlo, block_jhi, cum.reshape(-1), off.reshape(-1),
                     f, w_gate[l].astype(BF16), b_gate[l][:, None, :], w_up[l].astype(BF16), b_up[l][:, None, :],
                     w_down[l].astype(BF16), b_down[l][:, None, :], n_tiles, tile_rows)
        h = _combine(y, h2, pos, topg, row(final_norm_g), final=l == w_in.shape[0] - 1)
    return h.reshape(batch, seq, d)
```

```python
import functools

import numpy as np
import jax
import jax.numpy as jnp
from jax import lax
from jax.experimental import pallas as pl
from jax.experimental.pallas import tpu as pltpu

F32 = jnp.float32
BF16 = jnp.bfloat16
I32 = jnp.int32

EPS = 1e-5
ROPE_THETA = 10000.0
GLA_HEADS = 4
GLA_DK = 64
GLA_DV = 128
GLA_GATE_RANK = 16
GLA_GATE_TEMP = 16.0
GLA_CHUNK = 64
GLA_QK = GLA_HEADS * GLA_DK
GLA_V = GLA_HEADS * GLA_DV
DIL_PAIRS = ((128, 1), (512, 4), (2048, 16))
DIL_GROUPS = 3
DIL_HEADS = 4
DIL_HEAD_DIM = 128
DIL_BLOCK = 128
DIL_GW = DIL_HEADS * DIL_HEAD_DIM
DIL_W = DIL_GROUPS * DIL_GW
XATTN_HEADS = 4
XATTN_HEAD_DIM = 128
N_EXPERTS = 32
TOP_K = 4
SWIGLU_ALPHA = 1.702
SWIGLU_LIMIT = 7.0

LANES = 128
NEG_BIG = -1e30
VMEM_LIMIT = 56 * 1024 * 1024

TOKEN_TILE = 512
POST_TILE = 256
GLA_TILE = 512
MOE_ROWS = 256


def _params(*sem):
    return pltpu.CompilerParams(dimension_semantics=sem, vmem_limit_bytes=VMEM_LIMIT)


def _resident(shape):
    nd = len(shape)
    return pl.BlockSpec(shape, lambda *_: (0,) * nd, pipeline_mode=pl.Buffered(1))


def _rms(x, g):
    return x * lax.rsqrt(jnp.mean(x * x, axis=-1, keepdims=True) + EPS) * g


def _dot(a, b):
    return jnp.dot(a, b, preferred_element_type=F32)


def _dot_nt(a, b):
    return lax.dot_general(a, b, (((1,), (1,)), ((), ())), preferred_element_type=F32)


def _gla_proj_kernel(x_ref, g_ref, w_ref, walr_ref, wa2_ref, ba2_ref,
                     q_ref, k_ref, v_ref, r_ref, la_ref):
    a = _rms(x_ref[...], g_ref[...]).astype(BF16)
    y = _dot(a, w_ref[...])
    q_ref[...] = (y[:, :GLA_QK] * (GLA_DK ** -0.5)).astype(BF16)
    k_ref[...] = y[:, GLA_QK:2 * GLA_QK].astype(BF16)
    v_ref[...] = y[:, 2 * GLA_QK:2 * GLA_QK + GLA_V].astype(BF16)
    r = y[:, 2 * GLA_QK + GLA_V:]
    r_ref[...] = (r * jax.nn.sigmoid(r)).astype(BF16)
    a_lr = _dot(a, walr_ref[...])
    logit = _dot(a_lr.astype(BF16), wa2_ref[...]) + ba2_ref[...]
    log_sig = jnp.minimum(logit, 0.0) - jnp.log1p(jnp.exp(-jnp.abs(logit)))
    la_ref[...] = log_sig * (1.0 / GLA_GATE_TEMP)


def _gla_proj(x2, g, w, walr, wa2, ba2):
    m, d = x2.shape
    t = TOKEN_TILE
    row = lambda n: pl.BlockSpec((t, n), lambda i: (i, 0))
    return pl.pallas_call(
        _gla_proj_kernel,
        grid=(m // t,),
        in_specs=[row(d), _resident(g.shape), _resident(w.shape), _resident(walr.shape),
                  _resident(wa2.shape), _resident(ba2.shape)],
        out_specs=[row(GLA_QK), row(GLA_QK), row(GLA_V), row(GLA_V), row(GLA_QK)],
        out_shape=[jax.ShapeDtypeStruct((m, GLA_QK), BF16), jax.ShapeDtypeStruct((m, GLA_QK), BF16),
                   jax.ShapeDtypeStruct((m, GLA_V), BF16), jax.ShapeDtypeStruct((m, GLA_V), BF16),
                   jax.ShapeDtypeStruct((m, GLA_QK), F32)],
        compiler_params=_params("parallel"),
        name="gla_proj",
    )(x2, g, w, walr, wa2, ba2)


def _dil_proj_kernel(x_ref, pos_ref, g_ref, invf_ref, wq_ref, wk_ref, wv_ref, *refs):
    q_refs, k_refs, v_refs = refs[0:3], refs[3:6], refs[6:9]
    stage = refs[9:]
    a = _rms(x_ref[...], g_ref[...]).astype(BF16)
    t = a.shape[0]
    ang = pos_ref[...].astype(F32) * invf_ref[...]
    lane = lax.broadcasted_iota(I32, ang.shape, 1)
    first_half = lane < DIL_HEAD_DIM // 2
    cos_sin = jnp.cos(jnp.where(first_half, ang, ang - (np.pi / 2)))
    sin_cos = pltpu.roll(cos_sin, DIL_HEAD_DIM // 2, 1)
    cos = jnp.where(first_half, cos_sin, sin_cos)
    sin_signed = jnp.where(first_half, -sin_cos, cos_sin)

    def rope(v):
        return v * cos + pltpu.roll(v, DIL_HEAD_DIM // 2, 1) * sin_signed

    ys = (_dot(a, wq_ref[...]), _dot(a, wk_ref[...]), _dot(a, wv_ref[...]))
    post = (lambda v: rope(v) * (DIL_HEAD_DIM ** -0.5), rope, lambda v: v)
    for ti, out_refs in enumerate((q_refs, k_refs, v_refs)):
        for g, (_, dil) in enumerate(DIL_PAIRS):
            for h in range(DIL_HEADS):
                c0 = g * DIL_GW + h * DIL_HEAD_DIM
                hs = slice(h * DIL_HEAD_DIM, (h + 1) * DIL_HEAD_DIM)
                val = post[ti](ys[ti][:, c0:c0 + DIL_HEAD_DIM])
                if dil == 1:
                    out_refs[g][0, :, hs] = val.astype(BF16)
                else:
                    buf = stage[ti * (DIL_GROUPS - 1) + g - 1]
                    buf[h] = val
                    for p in range(dil):
                        out_refs[g][p, :, hs] = buf[h, pl.ds(p, t // dil, stride=dil), :].astype(BF16)


def _dil_proj(x2, pos2, g, invf, wq, wk, wv, batch, seq):
    m, d = x2.shape
    t = TOKEN_TILE
    nt = seq // t
    row = lambda n: pl.BlockSpec((t, n), lambda i: (i, 0))
    out_specs, out_shape = [], []
    for _ in range(3):
        for _, dil in DIL_PAIRS:
            out_specs.append(pl.BlockSpec((None, dil, t // dil, DIL_GW), lambda i: (i // nt, 0, i % nt, 0)))
            out_shape.append(jax.ShapeDtypeStruct((batch, dil, seq // dil, DIL_GW), BF16))
    return pl.pallas_call(
        _dil_proj_kernel,
        grid=(m // t,),
        in_specs=[row(d), row(1), _resident(g.shape), _resident(invf.shape),
                  _resident(wq.shape), _resident(wk.shape), _resident(wv.shape)],
        out_specs=out_specs,
        out_shape=out_shape,
        scratch_shapes=[pltpu.VMEM((DIL_HEADS, t, DIL_HEAD_DIM), F32)] * (3 * (DIL_GROUPS - 1)),
        compiler_params=_params("parallel"),
        name="dil_proj",
    )(x2, pos2, g, invf, wq, wk, wv)


GLA_LEVELS = (32, 16, 8, 4, 2, 1)
GLA_ONE_FACTOR_MIN = -60.0


def _gla_level_masks():
    c = GLA_CHUNK
    t = np.arange(c)[:, None]
    r = np.arange(c)[None, :]
    blocks = []
    for h in GLA_LEVELS:
        u = t % (2 * h)
        m = t - u + h - 1
        blocks.append(((u >= h) & (r > m) & (r <= t)).astype(np.float32))
        blocks.append(((u < h) & (r > t) & (r <= m)).astype(np.float32))
    return np.concatenate(blocks, axis=0)


def _gla_kernel(q_ref, k_ref, v_ref, r_ref, la_ref, gn_ref, tri_ref, lvl_ref, o_ref, st_ref, bc_ref):
    @pl.when(pl.program_id(1) == 0)
    def _():
        st_ref[...] = jnp.zeros_like(st_ref)

    c = GLA_CHUNK
    n_chunks = q_ref.shape[0] // c
    row_i = lax.broadcasted_iota(I32, (c, c), 0)
    col_i = lax.broadcasted_iota(I32, (c, c), 1)
    gn = gn_ref[...]

    tri = tri_ref[...]
    min_last = None
    for ci in range(n_chunks):
        rows = slice(ci * c, (ci + 1) * c)
        bcum = jnp.dot(tri, la_ref[rows, :], precision=lax.Precision.HIGHEST,
                       preferred_element_type=F32)
        bc_ref[rows, :] = bcum
        last = bcum[c - 1:c, :]
        min_last = last if min_last is None else jnp.minimum(min_last, last)
    one_factor_ok = jnp.min(min_last) >= GLA_ONE_FACTOR_MIN

    def finish_chunk(rows, q_dec, k_dec, e_last, v, r, scores_of_head):
        for h in range(GLA_HEADS):
            ks = slice(h * GLA_DK, (h + 1) * GLA_DK)
            vs = slice(h * GLA_DV, (h + 1) * GLA_DV)
            vh = v[:, vs]
            st = st_ref[h]
            o = _dot(scores_of_head(h, ks).astype(BF16), vh) + _dot_nt(q_dec[:, ks], st.astype(BF16))
            v_t = vh.astype(F32).T.astype(BF16)
            st_ref[h] = st * e_last[:, ks] + _dot(v_t, k_dec[:, ks])
            o = o * lax.rsqrt(jnp.mean(o * o, axis=-1, keepdims=True) + EPS) * gn[:, vs]
            o_ref[rows, vs] = (o * r[:, vs]).astype(BF16)

    def load_chunk(rows):
        q = q_ref[rows, :].astype(F32)
        k = k_ref[rows, :].astype(F32)
        bcum = bc_ref[rows, :]
        blast = bcum[c - 1:c, :]
        q_dec = (q * jnp.exp(bcum)).astype(BF16)
        k_dec = (k * jnp.exp(blast - bcum)).astype(BF16)
        return q, k, bcum, q_dec, k_dec, jnp.exp(blast), v_ref[rows, :], r_ref[rows, :].astype(F32)

    @pl.when(one_factor_ok)
    def _():
        for ci in range(n_chunks):
            rows = slice(ci * c, (ci + 1) * c)
            q, k, bcum, q_dec, k_dec, e_last, v, r = load_chunk(rows)
            k_inv = (k * jnp.exp(-bcum)).astype(BF16)
            finish_chunk(rows, q_dec, k_dec, e_last, v, r,
                         lambda h, ks: jnp.where(row_i >= col_i, _dot_nt(q_dec[:, ks], k_inv[:, ks]), 0.0))

    @pl.when(jnp.logical_not(one_factor_ok))
    def _():
        def chunk(ci, carry):
            rows = pl.ds(pl.multiple_of(ci * c, c), c)
            q, k, bcum, q_dec, k_dec, e_last, v, r = load_chunk(rows)
            expo = jnp.dot(lvl_ref[...], la_ref[rows, :], precision=lax.Precision.HIGHEST,
                           preferred_element_type=F32)
            q_lv = [(q * jnp.exp(expo[2 * j * c:(2 * j + 1) * c, :])).astype(BF16) for j in range(len(GLA_LEVELS))]
            k_lv = [(k * jnp.exp(expo[(2 * j + 1) * c:(2 * j + 2) * c, :])).astype(BF16) for j in range(len(GLA_LEVELS))]
            q_bf, k_bf = q.astype(BF16), k.astype(BF16)

            def scores_of_head(h, ks):
                sc = jnp.where(row_i == col_i, _dot_nt(q_bf[:, ks], k_bf[:, ks]), 0.0)
                differ = row_i ^ col_i
                for j, half in enumerate(GLA_LEVELS):
                    valid = (differ >= half) & (differ < 2 * half) & ((row_i & half) != 0)
                    sc = sc + jnp.where(valid, _dot_nt(q_lv[j][:, ks], k_lv[j][:, ks]), 0.0)
                return sc
            finish_chunk(rows, q_dec, k_dec, e_last, v, r, scores_of_head)
            return carry
        lax.fori_loop(0, n_chunks, chunk, 0)


def _gla(q, k, v, r, la, gn, batch, seq):
    t = GLA_TILE
    nt = seq // t
    row = lambda n: pl.BlockSpec((t, n), lambda b, j: (b * nt + j, 0))
    const = lambda a: pl.BlockSpec(a.shape, lambda b, j: (0, 0))
    tri = jnp.tril(jnp.ones((GLA_CHUNK, GLA_CHUNK), F32))
    lvl = jnp.asarray(_gla_level_masks())
    return pl.pallas_call(
        _gla_kernel,
        grid=(batch, nt),
        in_specs=[row(GLA_QK), row(GLA_QK), row(GLA_V), row(GLA_V), row(GLA_QK), const(gn), const(tri), const(lvl)],
        out_specs=row(GLA_V),
        out_shape=jax.ShapeDtypeStruct((batch * seq, GLA_V), BF16),
        scratch_shapes=[pltpu.VMEM((GLA_HEADS, GLA_DV, GLA_DK), F32), pltpu.VMEM((t, GLA_QK), F32)],
        compiler_params=_params("parallel", "arbitrary"),
        name="gla",
    )(q, k, v, r, la, gn, tri, lvl)


def _dil_attn_kernel(q_ref, k_ref, v_ref, o_ref, lse_ref, kp_ref, vp_ref):
    n = pl.program_id(2)

    @pl.when(n == 0)
    def _():
        kp_ref[...] = jnp.zeros_like(kp_ref)
        vp_ref[...] = jnp.zeros_like(vp_ref)

    qb = DIL_BLOCK
    row = lax.broadcasted_iota(I32, (qb, qb), 0)
    col = lax.broadcasted_iota(I32, (qb, qb), 1)
    lower = col <= row
    prev_bias = jnp.where(n > 0, 0.0, NEG_BIG)
    lane = lax.broadcasted_iota(I32, (qb, LANES), 1)
    for b in range(q_ref.shape[0]):
        lse_all = jnp.zeros((qb, LANES), F32)
        for h in range(DIL_HEADS):
            hs = slice(h * DIL_HEAD_DIM, (h + 1) * DIL_HEAD_DIM)
            q = q_ref[b, :, hs]
            s_cur = _dot_nt(q, k_ref[b, :, hs])
            s_prev = _dot_nt(q, kp_ref[b, :, hs])
            s = jnp.where(lower, s_cur, s_prev + prev_bias)
            s_far = jnp.sum(jnp.where(row == col, s_prev, 0.0), axis=-1, keepdims=True) + prev_bias
            m = jnp.maximum(jnp.max(s, axis=-1, keepdims=True), s_far)
            p = jnp.exp(s - m)
            p_far = jnp.exp(s_far - m)
            l = jnp.sum(p, axis=-1, keepdims=True) + p_far
            vp = vp_ref[b, :, hs]
            o = (_dot(jnp.where(lower, p, 0.0).astype(BF16), v_ref[b, :, hs])
                 + _dot(jnp.where(lower, 0.0, p).astype(BF16), vp) + p_far * vp.astype(F32))
            o_ref[b, :, hs] = (o / l).astype(BF16)
            lse_all = jnp.where(lane == h, m + jnp.log(l), lse_all)
        lse_ref[b] = lse_all
    kp_ref[...] = k_ref[...]
    vp_ref[...] = v_ref[...]


DIL_SEQS_PER_STEP = 4


def _dil_attn(q, k, v):
    batch, dilation, length, _ = q.shape
    nb = max(g for g in range(1, DIL_SEQS_PER_STEP + 1) if batch % g == 0)
    blk = lambda n: pl.BlockSpec((nb, None, DIL_BLOCK, n), lambda b, p, i: (b, p, i, 0))
    return pl.pallas_call(
        _dil_attn_kernel,
        grid=(batch // nb, dilation, length // DIL_BLOCK),
        in_specs=[blk(DIL_GW)] * 3,
        out_specs=[blk(DIL_GW), blk(LANES)],
        out_shape=[jax.ShapeDtypeStruct(q.shape, BF16),
                   jax.ShapeDtypeStruct((batch, dilation, length, LANES), F32)],
        scratch_shapes=[pltpu.VMEM((nb, DIL_BLOCK, DIL_GW), BF16), pltpu.VMEM((nb, DIL_BLOCK, DIL_GW), BF16)],
        compiler_params=_params("parallel", "parallel", "arbitrary"),
        name=f"dil_attn_d{dilation}",
    )(q, k, v)


def _mem_kv_kernel(mem_ref, g_ref, w_ref, kv_ref):
    kv_ref[...] = _dot(_rms(mem_ref[...], g_ref[...]).astype(BF16), w_ref[...]).astype(BF16)


def _mem_kv(mem, g, w):
    b, ml, d = mem.shape
    n = w.shape[1]
    return pl.pallas_call(
        _mem_kv_kernel,
        grid=(b,),
        in_specs=[pl.BlockSpec((None, ml, d), lambda i: (i, 0, 0)), _resident(g.shape), _resident(w.shape)],
        out_specs=pl.BlockSpec((None, ml, n), lambda i: (i, 0, 0)),
        out_shape=jax.ShapeDtypeStruct((b, ml, n), BF16),
        compiler_params=_params("parallel"),
        name="mem_kv",
    )(mem, g, w)


def _post_mixer_kernel(x_ref, og_ref, od0_ref, od1_ref, od2_ref, l0_ref, l1_ref, l2_ref, kv_ref,
                       g1_ref, wgate_ref, bgate_ref, wbg_ref, wbd_ref, wmix_ref,
                       g2_ref, wxq_ref, wxo_ref, g3_ref, wrh_ref, wrl_ref, br_ref,
                       h_ref, f_ref, pos_ref, topg_ref, cnt_ref,
                       o1_buf, o2_buf, l1_buf, l2_buf):
    x = x_ref[...]
    t, d = x.shape
    a = _rms(x, g1_ref[...]).astype(BF16)

    for od_ref, l_ref, o_buf, l_buf in ((od1_ref, l1_ref, o1_buf, l1_buf), (od2_ref, l2_ref, o2_buf, l2_buf)):
        dil = od_ref.shape[0]
        for p in range(dil):
            rows = pl.ds(p, t // dil, stride=dil)
            l_buf[rows, :] = l_ref[p]
            for h in range(DIL_HEADS):
                o_buf[h, rows, :] = od_ref[p, :, h * DIL_HEAD_DIM:(h + 1) * DIL_HEAD_DIM].astype(F32)

    l0, l1, l2 = l0_ref[0], l1_buf[...], l2_buf[...]
    lm = jnp.maximum(jnp.maximum(l0, l1), l2)
    e0, e1, e2 = jnp.exp(l0 - lm), jnp.exp(l1 - lm), jnp.exp(l2 - lm)
    inv = 1.0 / (e0 + e1 + e2)
    w0, w1, w2 = e0 * inv, e1 * inv, e2 * inv
    od_heads = []
    for h in range(DIL_HEADS):
        hs = slice(h * DIL_HEAD_DIM, (h + 1) * DIL_HEAD_DIM)
        od_heads.append(w0[:, h:h + 1] * od0_ref[0, :, hs].astype(F32)
                        + w1[:, h:h + 1] * o1_buf[h] + w2[:, h:h + 1] * o2_buf[h])
    o_d = jnp.concatenate(od_heads, axis=-1).astype(BF16)

    gates = jax.nn.sigmoid(_dot(a, wgate_ref[...]) + bgate_ref[...])
    merged = gates[:, :d] * _dot(og_ref[...], wbg_ref[...]) + gates[:, d:] * _dot(o_d, wbd_ref[...])
    h1 = x + _dot(merged.astype(BF16), wmix_ref[...])

    c = _rms(h1, g2_ref[...]).astype(BF16)
    qx = (_dot(c, wxq_ref[...]) * (XATTN_HEAD_DIM ** -0.5)).astype(BF16)
    xw = XATTN_HEADS * XATTN_HEAD_DIM
    heads = []
    for h in range(XATTN_HEADS):
        hs = slice(h * XATTN_HEAD_DIM, (h + 1) * XATTN_HEAD_DIM)
        vs = slice(xw + h * XATTN_HEAD_DIM, xw + (h + 1) * XATTN_HEAD_DIM)
        s = _dot_nt(qx[:, hs], kv_ref[:, hs])
        p = jnp.exp(s - jnp.max(s, axis=-1, keepdims=True))
        o = _dot(p.astype(BF16), kv_ref[:, vs]) / jnp.sum(p, axis=-1, keepdims=True)
        heads.append(o.astype(BF16))
    h2 = h1 + _dot(jnp.concatenate(heads, axis=-1), wxo_ref[...])
    h_ref[...] = h2

    f = _rms(h2, g3_ref[...])
    f_hi = f.astype(BF16)
    f_lo = (f - f_hi.astype(F32)).astype(BF16)
    logits = (_dot(f_hi, wrh_ref[...]) + (_dot(f_lo, wrh_ref[...]) + _dot(f_hi, wrl_ref[...]))) + br_ref[...]
    lane = lax.broadcasted_iota(I32, (t, LANES), 1)
    work = jnp.where(lane < N_EXPERTS, logits, NEG_BIG)
    selected = jnp.zeros((t, LANES), F32)
    top_val, top_idx = [], []
    for _ in range(TOP_K):
        mx = jnp.max(work, axis=-1, keepdims=True)
        idx = jnp.min(jnp.where(work == mx, lane, LANES), axis=-1, keepdims=True)
        hit = lane == idx
        selected = jnp.where(hit, 1.0, selected)
        work = jnp.where(hit, NEG_BIG, work)
        top_val.append(mx)
        top_idx.append(idx)
    ex = [jnp.exp(v - top_val[0]) for v in top_val]
    inv_den = 1.0 / (ex[0] + ex[1] + ex[2] + ex[3])

    sel16 = selected.astype(BF16)
    earlier = lax.broadcasted_iota(I32, (t, t), 1) < lax.broadcasted_iota(I32, (t, t), 0)
    before = _dot(jnp.where(earlier, 1.0, 0.0).astype(BF16), sel16)
    count = jnp.sum(selected, axis=0, keepdims=True)
    lower_expert = lax.broadcasted_iota(I32, (LANES, LANES), 0) < lax.broadcasted_iota(I32, (LANES, LANES), 1)
    offset = _dot(jnp.broadcast_to(count, (8, LANES)).astype(BF16),
                  jnp.where(lower_expert, 1.0, 0.0).astype(BF16))[0:1, :]
    where_to = before + offset
    pos = jnp.zeros((t, LANES), F32)
    topg = jnp.zeros((t, LANES), F32)
    for kk in range(TOP_K):
        pk = jnp.sum(jnp.where(lane == top_idx[kk], where_to, 0.0), axis=-1, keepdims=True)
        pos = jnp.where(lane == kk, pk, pos)
        topg = jnp.where(lane == kk, ex[kk] * inv_den, topg)
    pos_ref[...] = pos.astype(I32)
    topg_ref[...] = topg
    cnt_ref[...] = jnp.broadcast_to(count, cnt_ref.shape)

    pos_rows = pos.T
    slot_i = lax.broadcasted_iota(I32, (t * TOP_K, t), 0).astype(F32)
    perm = jnp.zeros((t * TOP_K, t), F32)
    for kk in range(TOP_K):
        perm = jnp.where(slot_i == pos_rows[kk:kk + 1, :], 1.0, perm)
    f_sorted = _dot(perm.astype(BF16), f_hi)
    for j in range(d // LANES):
        f_ref[pl.ds(j, t * TOP_K, stride=d // LANES), :] = f_sorted[:, j * LANES:(j + 1) * LANES]


def _post_mixer(x2, og, ods, lses, kv, seq, g1, wgate, bgate, wbg, wbd, wmix, g2, wxq, wxo, g3, wr, br):
    wr_hi = wr.astype(BF16)
    wr_lo = (wr - wr_hi.astype(F32)).astype(BF16)
    m, d = x2.shape
    t = POST_TILE
    nt = seq // t
    sub = d // LANES
    row = lambda n: pl.BlockSpec((t, n), lambda i: (i, 0))
    phase = lambda a: pl.BlockSpec((None, a.shape[1], t // a.shape[1], a.shape[3]),
                                   lambda i: (i // nt, 0, i % nt, 0))
    kv_spec = pl.BlockSpec((None,) + kv.shape[1:], lambda i: (i // nt, 0, 0))
    weights = [g1, wgate, bgate, wbg, wbd, wmix, g2, wxq, wxo, g3, wr_hi, wr_lo, br]
    return pl.pallas_call(
        _post_mixer_kernel,
        grid=(m // t,),
        in_specs=[row(d), row(GLA_V)] + [phase(a) for a in ods] + [phase(a) for a in lses] + [kv_spec]
                 + [_resident(w.shape) for w in weights],
        out_specs=[row(d), pl.BlockSpec((t * TOP_K * sub, LANES), lambda i: (i, 0)), row(LANES), row(LANES),
                   pl.BlockSpec((8, LANES), lambda i: (i, 0))],
        out_shape=[jax.ShapeDtypeStruct((m, d), F32), jax.ShapeDtypeStruct((m * TOP_K * sub, LANES), F32),
                   jax.ShapeDtypeStruct((m, LANES), I32), jax.ShapeDtypeStruct((m, LANES), F32),
                   jax.ShapeDtypeStruct((m // t * 8, LANES), F32)],
        scratch_shapes=[pltpu.VMEM((DIL_HEADS, t, DIL_HEAD_DIM), F32), pltpu.VMEM((DIL_HEADS, t, DIL_HEAD_DIM), F32),
                        pltpu.VMEM((t, LANES), F32), pltpu.VMEM((t, LANES), F32)],
        compiler_params=_params("parallel"),
        name="post_mixer",
    )(x2, og, *ods, *lses, kv, *weights)


ROW_SUB = 8
RUN_BITS = tuple(1 << b for b in range(8, -1, -1))
RUN_SMALL = 64


def _wait_tiles(hbm, vmem, sem):
    pltpu.make_async_copy(hbm.at[pl.ds(0, vmem.shape[0]), :], vmem, sem).wait()


def _experts_kernel(be_ref, nu_ref, r0_ref, nv_ref, jlo_ref, jhi_ref, cum_ref, off_ref,
                    f_hbm, wg_ref, bg_ref, wu_ref, bu_ref, wd_ref, bd_ref, y_hbm,
                    xbuf, ybuf, in_sems, out_sems, *, n_tiles, tile_rows):
    i = pl.program_id(0)
    n_used = nu_ref[0]
    rows = xbuf.shape[1] // ROW_SUB
    spare_row = n_tiles * tile_rows

    def copy_rows(hbm_row, buf_row, count, start_piece):
        def piece(bit):
            @pl.when((count & bit) != 0)
            def _():
                taken = count & (-2 * bit)
                start_piece(hbm_row + taken, buf_row + taken, bit)

        @pl.when(count >= RUN_SMALL)
        def _():
            for bit in RUN_BITS:
                if RUN_SMALL <= bit <= rows:
                    piece(bit)
        for bit in RUN_BITS:
            if bit < RUN_SMALL:
                piece(bit)

    def for_runs(b, start_piece, filler_row):
        e, r0, nv = be_ref[b], r0_ref[b], nv_ref[b]

        def body(j, carry):
            lo, hi = cum_ref[e * (n_tiles + 1) + j], cum_ref[e * (n_tiles + 1) + j + 1]
            first = jnp.maximum(lo, r0)
            count = jnp.maximum(jnp.minimum(hi, r0 + nv) - first, 0)
            copy_rows(j * tile_rows + off_ref[j * N_EXPERTS + e] + first - lo, first - r0, count, start_piece)
            return carry
        lax.fori_loop(jlo_ref[b], jhi_ref[b], body, 0)

        @pl.when(nv < rows)
        def _():
            copy_rows(filler_row, nv, rows - nv, start_piece)

    def gather(b, s):
        def piece(hbm_row, buf_row, n):
            pltpu.make_async_copy(f_hbm.at[pl.ds(hbm_row * ROW_SUB, n * ROW_SUB), :],
                                  xbuf.at[s, pl.ds(buf_row * ROW_SUB, n * ROW_SUB), :], in_sems.at[s]).start()
        for_runs(b, piece, 0)

    def scatter(b, s):
        def piece(hbm_row, buf_row, n):
            pltpu.make_async_copy(ybuf.at[s, pl.ds(buf_row * ROW_SUB, n * ROW_SUB), :],
                                  y_hbm.at[pl.ds(hbm_row * ROW_SUB, n * ROW_SUB), :], out_sems.at[s]).start()
        for_runs(b, piece, spare_row + s * rows)

    xs, ys = lax.rem(i, 2), lax.rem(i, 3)

    @pl.when(i == 0)
    def _():
        ybuf[2] = jnp.zeros(ybuf.shape[1:], F32)
        spare = [pltpu.make_async_copy(ybuf.at[2], y_hbm.at[pl.ds((spare_row + s * rows) * ROW_SUB, rows * ROW_SUB), :],
                                       out_sems.at[2]) for s in range(3)]
        for cp in spare:
            cp.start()
        for cp in spare:
            cp.wait()

    @pl.when((i >= 3) & (i - 3 < n_used))
    def _():
        _wait_tiles(y_hbm, ybuf.at[ys], out_sems.at[ys])

    @pl.when((i == 0) & (n_used > 0))
    def _():
        gather(0, 0)

    @pl.when(i + 1 < n_used)
    def _():
        gather(i + 1, lax.rem(i + 1, 2))

    @pl.when((i >= 1) & (i - 1 < n_used))
    def _():
        scatter(i - 1, lax.rem(i + 2, 3))

    @pl.when(i < n_used)
    def _():
        _wait_tiles(f_hbm, xbuf.at[xs], in_sems.at[xs])
        xb = jnp.concatenate([xbuf[xs, pl.ds(j, rows, stride=ROW_SUB), :] for j in range(ROW_SUB)],
                             axis=-1).astype(BF16)
        glu = jnp.minimum(_dot(xb, wg_ref[...]) + bg_ref[...], SWIGLU_LIMIT)
        lin = jnp.clip(_dot(xb, wu_ref[...]) + bu_ref[...], -SWIGLU_LIMIT, SWIGLU_LIMIT)
        act = glu * jax.nn.sigmoid(SWIGLU_ALPHA * glu) * (lin + 1.0)
        y = _dot(act.astype(BF16), wd_ref[...]) + bd_ref[...]
        for j in range(ROW_SUB):
            ybuf[ys, pl.ds(j, rows, stride=ROW_SUB), :] = y[:, j * LANES:(j + 1) * LANES]


def _experts(block_e, n_used, block_r0, block_nv, block_jlo, block_jhi, cum_flat, off_flat,
             f, wg, bg, wu, bu, wd, bd, n_tiles, tile_rows):
    rows = MOE_ROWS
    n_blocks = block_e.shape[0]
    wspec = lambda w: pl.BlockSpec((None,) + w.shape[1:],
                                   lambda i, be, *_: (be[jnp.minimum(i, n_blocks - 1)], 0, 0))
    return pl.pallas_call(
        functools.partial(_experts_kernel, n_tiles=n_tiles, tile_rows=tile_rows),
        grid_spec=pltpu.PrefetchScalarGridSpec(
            num_scalar_prefetch=8,
            grid=(n_blocks + 3,),
            in_specs=[pl.BlockSpec(memory_space=pl.ANY),
                      wspec(wg), wspec(bg), wspec(wu), wspec(bu), wspec(wd), wspec(bd)],
            out_specs=pl.BlockSpec(memory_space=pl.ANY),
            scratch_shapes=[pltpu.VMEM((2, rows * ROW_SUB, LANES), F32), pltpu.VMEM((3, rows * ROW_SUB, LANES), F32),
                            pltpu.SemaphoreType.DMA((2,)), pltpu.SemaphoreType.DMA((3,))],
        ),
        out_shape=jax.ShapeDtypeStruct(((n_tiles * tile_rows + 3 * rows) * ROW_SUB, LANES), F32),
        compiler_params=_params("arbitrary"),
        name="experts",
    )(block_e, n_used, block_r0, block_nv, block_jlo, block_jhi, cum_flat, off_flat, f, wg, bg, wu, bu, wd, bd)


def _combine_kernel(y_ref, h_ref, pos_ref, topg_ref, g_ref, o_ref, *, final):
    t, d = h_ref.shape
    n_sorted = t * TOP_K
    pos = pos_ref[...]
    gate = topg_ref[...]
    slot_i = lax.broadcasted_iota(I32, (t, n_sorted), 1)
    w = jnp.zeros((t, n_sorted), F32)
    for kk in range(TOP_K):
        w = jnp.where(slot_i == pos[:, kk:kk + 1], gate[:, kk:kk + 1], w)
    y = jnp.concatenate([y_ref[pl.ds(j, n_sorted, stride=ROW_SUB), :] for j in range(ROW_SUB)], axis=-1)
    acc = h_ref[...] + _dot(w.astype(BF16), y.astype(BF16))
    o_ref[...] = _rms(acc, g_ref[...]) if final else acc


def _combine(y, h, pos, topg, g, final):
    m, d = h.shape
    t = POST_TILE
    row = lambda n: pl.BlockSpec((t, n), lambda i: (i, 0))
    return pl.pallas_call(
        functools.partial(_combine_kernel, final=final),
        grid=(m // t,),
        in_specs=[pl.BlockSpec((t * TOP_K * ROW_SUB, LANES), lambda i: (i, 0)), row(d), row(LANES), row(LANES),
                  pl.BlockSpec(g.shape, lambda i: (0, 0))],
        out_specs=row(d),
        out_shape=jax.ShapeDtypeStruct((m, d), F32),
        compiler_params=_params("parallel"),
        name="combine",
    )(y, h, pos, topg, g)


def _pad_lanes(w, axis):
    pad = [(0, 0)] * w.ndim
    pad[axis] = (0, LANES - w.shape[axis])
    return jnp.pad(w, pad)


def kernel(x, mem, positions, norm1_g, w_in, w_gla_a2, b_gla_a2, gla_norm_g, w_branch_gla, w_branch_dil, w_branch_gate, b_branch_gate, w_mix_out, norm2_g, mem_norm_g, w_xq, w_xkv, w_xo, norm3_g, w_router, b_router, w_gate, b_gate, w_up, b_up, w_down, b_down, final_norm_g):
    batch, seq, d = x.shape
    m = batch * seq
    h = x.reshape(m, d)
    pos2 = positions.reshape(m, 1)
    row = lambda v: v.reshape(1, -1)
    half = DIL_HEAD_DIM // 2
    inv_freq = ROPE_THETA ** (-jnp.arange(half, dtype=F32) / half)
    invf = jnp.concatenate([inv_freq, inv_freq]).reshape(1, DIL_HEAD_DIM)
    off_alr = 2 * GLA_QK + 2 * GLA_V
    off_q = off_alr + GLA_GATE_RANK

    for l in range(w_in.shape[0]):
        w_l = w_in[l]
        g1 = row(norm1_g[l])
        q_g, k_g, v_g, r_g, log_a = _gla_proj(
            h, g1, w_l[:, :off_alr].astype(BF16), _pad_lanes(w_l[:, off_alr:off_q], 1).astype(BF16),
            _pad_lanes(w_gla_a2[l], 0).astype(BF16), row(b_gla_a2[l]))
        dil = _dil_proj(h, pos2, g1, invf, w_l[:, off_q:off_q + DIL_W].astype(BF16),
                        w_l[:, off_q + DIL_W:off_q + 2 * DIL_W].astype(BF16),
                        w_l[:, off_q + 2 * DIL_W:].astype(BF16), batch, seq)
        o_g = _gla(q_g, k_g, v_g, r_g, log_a, row(gla_norm_g[l]), batch, seq)
        ods, lses = [], []
        for g in range(DIL_GROUPS):
            o_i, lse_i = _dil_attn(dil[g], dil[DIL_GROUPS + g], dil[2 * DIL_GROUPS + g])
            ods.append(o_i)
            lses.append(lse_i)
        kv = _mem_kv(mem, row(mem_norm_g[l]), w_xkv[l].astype(BF16))
        h2, f, pos, topg, cnt = _post_mixer(
            h, o_g, ods, lses, kv, seq, g1, w_branch_gate[l].astype(BF16), row(b_branch_gate[l]),
            w_branch_gla[l].astype(BF16), w_branch_dil[l].astype(BF16), w_mix_out[l].astype(BF16),
            row(norm2_g[l]), w_xq[l].astype(BF16), w_xo[l].astype(BF16), row(norm3_g[l]),
            _pad_lanes(w_router[l], 1), _pad_lanes(row(b_router[l]), 1))

        rows = MOE_ROWS
        n_tiles = m // POST_TILE
        tile_rows = POST_TILE * TOP_K
        count = cnt.reshape(n_tiles, 8, LANES)[:, 0, :N_EXPERTS].astype(I32)
        cum = jnp.concatenate([jnp.zeros((1, N_EXPERTS), I32), jnp.cumsum(count, axis=0)], axis=0).T
        off = jnp.cumsum(count, axis=1) - count
        total = cum[:, -1]
        padded = ((total + rows - 1) // rows) * rows
        pend = jnp.cumsum(padded)
        pstart = pend - padded
        n_blocks = -(-(m * TOP_K) // rows) + N_EXPERTS
        block_start = jnp.arange(n_blocks, dtype=I32) * rows
        block_e = jnp.minimum(jnp.sum((block_start[:, None] >= pend[None, :]).astype(I32), axis=1), N_EXPERTS - 1)
        n_used = (pend[-1:] // rows).astype(I32)
        block_r0 = block_start - pstart[block_e]
        block_nv = jnp.clip(total[block_e] - block_r0, 0, rows)
        cum_b = cum[block_e]
        block_jlo = jnp.sum((cum_b[:, 1:] <= block_r0[:, None]).astype(I32), axis=1)
        block_jhi = jnp.sum((cum_b[:, :-1] < (block_r0 + block_nv)[:, None]).astype(I32), axis=1)
        y = _experts(block_e, n_used, block_r0, block_nv, block_jlo, block_jhi, cum.reshape(-1), off.reshape(-1),
                     f, w_gate[l].astype(BF16), b_gate[l][:, None, :], w_up[l].astype(BF16), b_up[l][:, None, :],
                     w_down[l].astype(BF16), b_down[l][:, None, :], n_tiles, tile_rows)
        h = _combine(y, h2, pos, topg, row(final_norm_g), final=l == w_in.shape[0] - 1)
    return h.reshape(batch, seq, d)
```

```python
import functools

import numpy as np
import jax
import jax.numpy as jnp
from jax import lax
from jax.experimental import pallas as pl
from jax.experimental.pallas import tpu as pltpu

F32 = jnp.float32
BF16 = jnp.bfloat16
I32 = jnp.int32

EPS = 1e-5
ROPE_THETA = 10000.0
GLA_HEADS = 4
GLA_DK = 64
GLA_DV = 128
GLA_GATE_RANK = 16
GLA_GATE_TEMP = 16.0
GLA_CHUNK = 64
GLA_QK = GLA_HEADS * GLA_DK
GLA_V = GLA_HEADS * GLA_DV
DIL_PAIRS = ((128, 1), (512, 4), (2048, 16))
DIL_GROUPS = 3
DIL_HEADS = 4
DIL_HEAD_DIM = 128
DIL_BLOCK = 128
DIL_GW = DIL_HEADS * DIL_HEAD_DIM
DIL_W = DIL_GROUPS * DIL_GW
XATTN_HEADS = 4
XATTN_HEAD_DIM = 128
N_EXPERTS = 32
TOP_K = 4
SWIGLU_ALPHA = 1.702
SWIGLU_LIMIT = 7.0

LANES = 128
NEG_BIG = -1e30
VMEM_LIMIT = 56 * 1024 * 1024

TOKEN_TILE = 512
POST_TILE = 256
GLA_TILE = 512
MOE_ROWS = 256
MERGE_COLS = 256


def _params(*sem):
    return pltpu.CompilerParams(dimension_semantics=sem, vmem_limit_bytes=VMEM_LIMIT)


def _resident(shape):
    nd = len(shape)
    return pl.BlockSpec(shape, lambda *_: (0,) * nd, pipeline_mode=pl.Buffered(1))


def _rms(x, g):
    return x * lax.rsqrt(jnp.mean(x * x, axis=-1, keepdims=True) + EPS) * g


def _dot(a, b):
    return jnp.dot(a, b, preferred_element_type=F32)


def _dot_nt(a, b):
    return lax.dot_general(a, b, (((1,), (1,)), ((), ())), preferred_element_type=F32)


def _gla_proj_kernel(x_ref, g_ref, w_ref, walr_ref, wa2_ref, ba2_ref,
                     q_ref, k_ref, v_ref, r_ref, la_ref):
    a = _rms(x_ref[...], g_ref[...]).astype(BF16)
    y = _dot(a, w_ref[...])
    q_ref[...] = (y[:, :GLA_QK] * (GLA_DK ** -0.5)).astype(BF16)
    k_ref[...] = y[:, GLA_QK:2 * GLA_QK].astype(BF16)
    v_ref[...] = y[:, 2 * GLA_QK:2 * GLA_QK + GLA_V].astype(BF16)
    r = y[:, 2 * GLA_QK + GLA_V:]
    r_ref[...] = (r * jax.nn.sigmoid(r)).astype(BF16)
    a_lr = _dot(a, walr_ref[...])
    logit = _dot(a_lr.astype(BF16), wa2_ref[...]) + ba2_ref[...]
    log_sig = jnp.minimum(logit, 0.0) - jnp.log1p(jnp.exp(-jnp.abs(logit)))
    la_ref[...] = log_sig * (1.0 / GLA_GATE_TEMP)


def _gla_proj(x2, g, w, walr, wa2, ba2):
    m, d = x2.shape
    t = TOKEN_TILE
    row = lambda n: pl.BlockSpec((t, n), lambda i: (i, 0))
    return pl.pallas_call(
        _gla_proj_kernel,
        grid=(m // t,),
        in_specs=[row(d), _resident(g.shape), _resident(w.shape), _resident(walr.shape),
                  _resident(wa2.shape), _resident(ba2.shape)],
        out_specs=[row(GLA_QK), row(GLA_QK), row(GLA_V), row(GLA_V), row(GLA_QK)],
        out_shape=[jax.ShapeDtypeStruct((m, GLA_QK), BF16), jax.ShapeDtypeStruct((m, GLA_QK), BF16),
                   jax.ShapeDtypeStruct((m, GLA_V), BF16), jax.ShapeDtypeStruct((m, GLA_V), BF16),
                   jax.ShapeDtypeStruct((m, GLA_QK), F32)],
        compiler_params=_params("parallel"),
        name="gla_proj",
    )(x2, g, w, walr, wa2, ba2)


def _dil_proj_kernel(x_ref, pos_ref, g_ref, invf_ref, wq_ref, wk_ref, wv_ref, *refs):
    q_refs, k_refs, v_refs = refs[0:3], refs[3:6], refs[6:9]
    stage = refs[9:]
    a = _rms(x_ref[...], g_ref[...]).astype(BF16)
    t = a.shape[0]
    ang = pos_ref[...].astype(F32) * invf_ref[...]
    lane = lax.broadcasted_iota(I32, ang.shape, 1)
    first_half = lane < DIL_HEAD_DIM // 2
    cos_sin = jnp.cos(jnp.where(first_half, ang, ang - (np.pi / 2)))
    sin_cos = pltpu.roll(cos_sin, DIL_HEAD_DIM // 2, 1)
    cos = jnp.where(first_half, cos_sin, sin_cos)
    sin_signed = jnp.where(first_half, -sin_cos, cos_sin)

    def rope(v):
        return v * cos + pltpu.roll(v, DIL_HEAD_DIM // 2, 1) * sin_signed

    post = (lambda v: rope(v) * (DIL_HEAD_DIM ** -0.5), rope, lambda v: v)
    pair = 2 * DIL_HEAD_DIM
    for ti, (w_ref, out_refs) in enumerate(((wq_ref, q_refs), (wk_ref, k_refs), (wv_ref, v_refs))):
        for g, (_, dil) in enumerate(DIL_PAIRS):
            for h2 in range(DIL_HEADS // 2):
                y = _dot(a, w_ref[:, g * DIL_GW + h2 * pair:g * DIL_GW + (h2 + 1) * pair])
                for h in (2 * h2, 2 * h2 + 1):
                    hs = slice(h * DIL_HEAD_DIM, (h + 1) * DIL_HEAD_DIM)
                    val = post[ti](y[:, (h - 2 * h2) * DIL_HEAD_DIM:(h - 2 * h2 + 1) * DIL_HEAD_DIM])
                    if dil == 1:
                        out_refs[g][0, :, hs] = val.astype(BF16)
                    else:
                        buf = stage[ti * (DIL_GROUPS - 1) + g - 1]
                        buf[h] = val
                        for p in range(dil):
                            out_refs[g][p, :, hs] = buf[h, pl.ds(p, t // dil, stride=dil), :].astype(BF16)


def _dil_proj(x2, pos2, g, invf, wq, wk, wv, batch, seq):
    m, d = x2.shape
    t = TOKEN_TILE
    nt = seq // t
    row = lambda n: pl.BlockSpec((t, n), lambda i: (i, 0))
    out_specs, out_shape = [], []
    for _ in range(3):
        for _, dil in DIL_PAIRS:
            out_specs.append(pl.BlockSpec((None, dil, t // dil, DIL_GW), lambda i: (i // nt, 0, i % nt, 0)))
            out_shape.append(jax.ShapeDtypeStruct((batch, dil, seq // dil, DIL_GW), BF16))
    return pl.pallas_call(
        _dil_proj_kernel,
        grid=(m // t,),
        in_specs=[row(d), row(1), _resident(g.shape), _resident(invf.shape),
                  _resident(wq.shape), _resident(wk.shape), _resident(wv.shape)],
        out_specs=out_specs,
        out_shape=out_shape,
        scratch_shapes=[pltpu.VMEM((DIL_HEADS, t, DIL_HEAD_DIM), F32)] * (3 * (DIL_GROUPS - 1)),
        compiler_params=_params("parallel"),
        name="dil_proj",
    )(x2, pos2, g, invf, wq, wk, wv)


GLA_LEVELS = (32, 16, 8, 4, 2, 1)
GLA_ONE_FACTOR_MIN = -60.0


def _gla_level_masks():
    c = GLA_CHUNK
    t = np.arange(c)[:, None]
    r = np.arange(c)[None, :]
    blocks = []
    for h in GLA_LEVELS:
        u = t % (2 * h)
        m = t - u + h - 1
        blocks.append(((u >= h) & (r > m) & (r <= t)).astype(np.float32))
        blocks.append(((u < h) & (r > t) & (r <= m)).astype(np.float32))
    return np.concatenate(blocks, axis=0)


def _gla_kernel(q_ref, k_ref, v_ref, r_ref, la_ref, gn_ref, tri_ref, lvl_ref, o_ref, st_ref, bc_ref):
    @pl.when(pl.program_id(1) == 0)
    def _():
        st_ref[...] = jnp.zeros_like(st_ref)

    c = GLA_CHUNK
    n_chunks = q_ref.shape[0] // c
    row_i = lax.broadcasted_iota(I32, (c, c), 0)
    col_i = lax.broadcasted_iota(I32, (c, c), 1)
    gn = gn_ref[...]

    tri = tri_ref[...]
    min_last = None
    for ci in range(n_chunks):
        rows = slice(ci * c, (ci + 1) * c)
        bcum = jnp.dot(tri, la_ref[rows, :], precision=lax.Precision.HIGHEST,
                       preferred_element_type=F32)
        bc_ref[rows, :] = bcum
        last = bcum[c - 1:c, :]
        min_last = last if min_last is None else jnp.minimum(min_last, last)
    one_factor_ok = jnp.min(min_last) >= GLA_ONE_FACTOR_MIN

    def finish_chunk(rows, q_dec, k_dec, e_last, v, r, scores_of_head):
        for h in range(GLA_HEADS):
            ks = slice(h * GLA_DK, (h + 1) * GLA_DK)
            vs = slice(h * GLA_DV, (h + 1) * GLA_DV)
            vh = v[:, vs]
            st = st_ref[h]
            o = _dot(scores_of_head(h, ks).astype(BF16), vh) + _dot_nt(q_dec[:, ks], st.astype(BF16))
            v_t = vh.astype(F32).T.astype(BF16)
            st_ref[h] = st * e_last[:, ks] + _dot(v_t, k_dec[:, ks])
            o = o * lax.rsqrt(jnp.mean(o * o, axis=-1, keepdims=True) + EPS) * gn[:, vs]
            o_ref[rows, vs] = (o * r[:, vs]).astype(BF16)

    def load_chunk(rows):
        q = q_ref[rows, :].astype(F32)
        k = k_ref[rows, :].astype(F32)
        bcum = bc_ref[rows, :]
        blast = bcum[c - 1:c, :]
        q_dec = (q * jnp.exp(bcum)).astype(BF16)
        k_dec = (k * jnp.exp(blast - bcum)).astype(BF16)
        return q, k, bcum, q_dec, k_dec, jnp.exp(blast), v_ref[rows, :], r_ref[rows, :].astype(F32)

    @pl.when(one_factor_ok)
    def _():
        for ci in range(n_chunks):
            rows = slice(ci * c, (ci + 1) * c)
            q, k, bcum, q_dec, k_dec, e_last, v, r = load_chunk(rows)
            k_inv = (k * jnp.exp(-bcum)).astype(BF16)
            finish_chunk(rows, q_dec, k_dec, e_last, v, r,
                         lambda h, ks: jnp.where(row_i >= col_i, _dot_nt(q_dec[:, ks], k_inv[:, ks]), 0.0))

    @pl.when(jnp.logical_not(one_factor_ok))
    def _():
        def chunk(ci, carry):
            rows = pl.ds(pl.multiple_of(ci * c, c), c)
            q, k, bcum, q_dec, k_dec, e_last, v, r = load_chunk(rows)
            expo = jnp.dot(lvl_ref[...], la_ref[rows, :], precision=lax.Precision.HIGHEST,
                           preferred_element_type=F32)
            q_lv = [(q * jnp.exp(expo[2 * j * c:(2 * j + 1) * c, :])).astype(BF16) for j in range(len(GLA_LEVELS))]
            k_lv = [(k * jnp.exp(expo[(2 * j + 1) * c:(2 * j + 2) * c, :])).astype(BF16) for j in range(len(GLA_LEVELS))]
            q_bf, k_bf = q.astype(BF16), k.astype(BF16)

            def scores_of_head(h, ks):
                sc = jnp.where(row_i == col_i, _dot_nt(q_bf[:, ks], k_bf[:, ks]), 0.0)
                differ = row_i ^ col_i
                for j, half in enumerate(GLA_LEVELS):
                    valid = (differ >= half) & (differ < 2 * half) & ((row_i & half) != 0)
                    sc = sc + jnp.where(valid, _dot_nt(q_lv[j][:, ks], k_lv[j][:, ks]), 0.0)
                return sc
            finish_chunk(rows, q_dec, k_dec, e_last, v, r, scores_of_head)
            return carry
        lax.fori_loop(0, n_chunks, chunk, 0)


def _gla(q, k, v, r, la, gn, batch, seq):
    t = GLA_TILE
    nt = seq // t
    row = lambda n: pl.BlockSpec((t, n), lambda b, j: (b * nt + j, 0))
    const = lambda a: pl.BlockSpec(a.shape, lambda b, j: (0, 0))
    tri = jnp.tril(jnp.ones((GLA_CHUNK, GLA_CHUNK), F32))
    lvl = jnp.asarray(_gla_level_masks())
    return pl.pallas_call(
        _gla_kernel,
        grid=(batch, nt),
        in_specs=[row(GLA_QK), row(GLA_QK), row(GLA_V), row(GLA_V), row(GLA_QK), const(gn), const(tri), const(lvl)],
        out_specs=row(GLA_V),
        out_shape=jax.ShapeDtypeStruct((batch * seq, GLA_V), BF16),
        scratch_shapes=[pltpu.VMEM((GLA_HEADS, GLA_DV, GLA_DK), F32), pltpu.VMEM((t, GLA_QK), F32)],
        compiler_params=_params("parallel", "arbitrary"),
        name="gla",
    )(q, k, v, r, la, gn, tri, lvl)


def _dil_attn_kernel(q_ref, k_ref, v_ref, o_ref, lse_ref, kp_ref, vp_ref):
    n = pl.program_id(2)

    @pl.when(n == 0)
    def _():
        kp_ref[...] = jnp.zeros_like(kp_ref)
        vp_ref[...] = jnp.zeros_like(vp_ref)

    qb = DIL_BLOCK
    n_blk = q_ref.shape[0] // qb
    row = lax.broadcasted_iota(I32, (qb, qb), 0)
    col = lax.broadcasted_iota(I32, (qb, qb), 1)
    lower = col <= row
    prev_bias = jnp.where(n > 0, 0.0, NEG_BIG)
    lane = lax.broadcasted_iota(I32, (qb, LANES), 1)
    blk = lambda j: slice(j * qb, (j + 1) * qb)
    lse_all = [jnp.zeros((qb, LANES), F32) for _ in range(n_blk)]
    for h in range(DIL_HEADS):
        hs = slice(h * DIL_HEAD_DIM, (h + 1) * DIL_HEAD_DIM)
        s_prev = [_dot_nt(q_ref[blk(0), hs], kp_ref[:, hs])] + [None] * (n_blk - 1)
        s_cur = [None] * n_blk
        for j in range(n_blk):
            last = min(j + 2, n_blk)
            s = _dot_nt(q_ref[j * qb:last * qb, hs], k_ref[blk(j), hs])
            s_cur[j] = s[:qb]
            if j + 1 < n_blk:
                s_prev[j + 1] = s[qb:]
        p_lower, p_upper, p_far, denom = [], [], [], []
        for j in range(n_blk):
            bias = prev_bias if j == 0 else 0.0
            s = jnp.where(lower, s_cur[j], s_prev[j] + bias)
            s_far = jnp.sum(jnp.where(row == col, s_prev[j], 0.0), axis=-1, keepdims=True) + bias
            m = jnp.maximum(jnp.max(s, axis=-1, keepdims=True), s_far)
            p = jnp.exp(s - m)
            pf = jnp.exp(s_far - m)
            l = jnp.sum(p, axis=-1, keepdims=True) + pf
            p_lower.append(jnp.where(lower, p, 0.0).astype(BF16))
            p_upper.append(jnp.where(lower, 0.0, p).astype(BF16))
            p_far.append(pf)
            denom.append(l)
            lse_all[j] = jnp.where(lane == h, m + jnp.log(l), lse_all[j])
        vp = vp_ref[:, hs]
        acc = [_dot(p_upper[0], vp) + p_far[0] * vp.astype(F32)]
        acc += [p_far[j] * v_ref[blk(j - 1), hs].astype(F32) for j in range(1, n_blk)]
        for j in range(n_blk):
            if j + 1 < n_blk:
                o2 = _dot(jnp.concatenate([p_lower[j], p_upper[j + 1]], axis=0), v_ref[blk(j), hs])
                acc[j] = acc[j] + o2[:qb]
                acc[j + 1] = acc[j + 1] + o2[qb:]
            else:
                acc[j] = acc[j] + _dot(p_lower[j], v_ref[blk(j), hs])
        for j in range(n_blk):
            o_ref[blk(j), hs] = (acc[j] / denom[j]).astype(BF16)
    for j in range(n_blk):
        lse_ref[blk(j), :] = lse_all[j]
    kp_ref[...] = k_ref[blk(n_blk - 1), :]
    vp_ref[...] = v_ref[blk(n_blk - 1), :]


DIL_BLOCKS_PER_STEP = 4


def _dil_attn(q, k, v):
    batch, dilation, length, _ = q.shape
    n_blk = max(g for g in range(1, DIL_BLOCKS_PER_STEP + 1) if (length // DIL_BLOCK) % g == 0)
    rows = n_blk * DIL_BLOCK
    blk = lambda n: pl.BlockSpec((None, None, rows, n), lambda b, p, i: (b, p, i, 0))
    return pl.pallas_call(
        _dil_attn_kernel,
        grid=(batch, dilation, length // rows),
        in_specs=[blk(DIL_GW)] * 3,
        out_specs=[blk(DIL_GW), blk(LANES)],
        out_shape=[jax.ShapeDtypeStruct(q.shape, BF16),
                   jax.ShapeDtypeStruct((batch, dilation, length, LANES), F32)],
        scratch_shapes=[pltpu.VMEM((DIL_BLOCK, DIL_GW), BF16), pltpu.VMEM((DIL_BLOCK, DIL_GW), BF16)],
        compiler_params=_params("parallel", "parallel", "arbitrary"),
        name=f"dil_attn_d{dilation}",
    )(q, k, v)


def _mem_kv_kernel(mem_ref, g_ref, w_ref, kv_ref):
    kv_ref[...] = _dot(_rms(mem_ref[...], g_ref[...]).astype(BF16), w_ref[...]).astype(BF16)


def _mem_kv(mem, g, w):
    b, ml, d = mem.shape
    n = w.shape[1]
    return pl.pallas_call(
        _mem_kv_kernel,
        grid=(b,),
        in_specs=[pl.BlockSpec((None, ml, d), lambda i: (i, 0, 0)), _resident(g.shape), _resident(w.shape)],
        out_specs=pl.BlockSpec((None, ml, n), lambda i: (i, 0, 0)),
        out_shape=jax.ShapeDtypeStruct((b, ml, n), BF16),
        compiler_params=_params("parallel"),
        name="mem_kv",
    )(mem, g, w)


def _post_mixer_kernel(x_ref, og_ref, od0_ref, od1_ref, od2_ref, l0_ref, l1_ref, l2_ref, kv_ref,
                       g1_ref, wgate_ref, bgate_ref, wbg_ref, wbd_ref, wmix_ref,
                       g2_ref, wxq_ref, wxo_ref, g3_ref, wrh_ref, wrl_ref, br_ref,
                       h_ref, f_ref, pos_ref, topg_ref, cnt_ref,
                       o1_buf, o2_buf, l1_buf, l2_buf):
    x = x_ref[...]
    t, d = x.shape
    a = _rms(x, g1_ref[...]).astype(BF16)

    for od_ref, l_ref, o_buf, l_buf in ((od1_ref, l1_ref, o1_buf, l1_buf), (od2_ref, l2_ref, o2_buf, l2_buf)):
        dil = od_ref.shape[0]
        for p in range(dil):
            rows = pl.ds(p, t // dil, stride=dil)
            l_buf[rows, :] = l_ref[p]
            for h in range(DIL_HEADS):
                o_buf[h, rows, :] = od_ref[p, :, h * DIL_HEAD_DIM:(h + 1) * DIL_HEAD_DIM].astype(F32)

    l0, l1, l2 = l0_ref[0], l1_buf[...], l2_buf[...]
    lm = jnp.maximum(jnp.maximum(l0, l1), l2)
    e0, e1, e2 = jnp.exp(l0 - lm), jnp.exp(l1 - lm), jnp.exp(l2 - lm)
    inv = 1.0 / (e0 + e1 + e2)
    w0, w1, w2 = e0 * inv, e1 * inv, e2 * inv
    od_heads = []
    for h in range(DIL_HEADS):
        hs = slice(h * DIL_HEAD_DIM, (h + 1) * DIL_HEAD_DIM)
        od_heads.append(w0[:, h:h + 1] * od0_ref[0, :, hs].astype(F32)
                        + w1[:, h:h + 1] * o1_buf[h] + w2[:, h:h + 1] * o2_buf[h])
    o_d = jnp.concatenate(od_heads, axis=-1).astype(BF16)

    og = og_ref[...]
    merged = []
    for c0 in range(0, d, MERGE_COLS):
        cs, cs_dil = slice(c0, c0 + MERGE_COLS), slice(d + c0, d + c0 + MERGE_COLS)
        gate_gla = jax.nn.sigmoid(_dot(a, wgate_ref[:, cs]) + bgate_ref[:, cs])
        gate_dil = jax.nn.sigmoid(_dot(a, wgate_ref[:, cs_dil]) + bgate_ref[:, cs_dil])
        merged.append((gate_gla * _dot(og, wbg_ref[:, cs]) + gate_dil * _dot(o_d, wbd_ref[:, cs])).astype(BF16))
    h1 = x + _dot(jnp.concatenate(merged, axis=-1), wmix_ref[...])

    c = _rms(h1, g2_ref[...]).astype(BF16)
    qx = (_dot(c, wxq_ref[...]) * (XATTN_HEAD_DIM ** -0.5)).astype(BF16)
    xw = XATTN_HEADS * XATTN_HEAD_DIM
    heads = []
    for h in range(XATTN_HEADS):
        hs = slice(h * XATTN_HEAD_DIM, (h + 1) * XATTN_HEAD_DIM)
        vs = slice(xw + h * XATTN_HEAD_DIM, xw + (h + 1) * XATTN_HEAD_DIM)
        s = _dot_nt(qx[:, hs], kv_ref[:, hs])
        p = jnp.exp(s - jnp.max(s, axis=-1, keepdims=True))
        o = _dot(p.astype(BF16), kv_ref[:, vs]) / jnp.sum(p, axis=-1, keepdims=True)
        heads.append(o.astype(BF16))
    h2 = h1 + _dot(jnp.concatenate(heads, axis=-1), wxo_ref[...])
    h_ref[...] = h2

    f = _rms(h2, g3_ref[...])
    f_hi = f.astype(BF16)
    f_lo = (f - f_hi.astype(F32)).astype(BF16)
    logits = (_dot(f_hi, wrh_ref[...]) + (_dot(f_lo, wrh_ref[...]) + _dot(f_hi, wrl_ref[...]))) + br_ref[...]
    lane = lax.broadcasted_iota(I32, (t, LANES), 1)
    work = jnp.where(lane < N_EXPERTS, logits, NEG_BIG)
    selected = jnp.zeros((t, LANES), F32)
    top_val, top_idx = [], []
    lane_f = lane.astype(F32)
    for _ in range(TOP_K):
        mx = jnp.max(work, axis=-1, keepdims=True)
        idx = jnp.min(jnp.where(work == mx, lane_f, float(LANES)), axis=-1, keepdims=True)
        hit = lane_f == idx
        selected = jnp.where(hit, 1.0, selected)
        work = jnp.where(hit, NEG_BIG, work)
        top_val.append(mx)
        top_idx.append(idx)
    ex = [jnp.exp(v - top_val[0]) for v in top_val]
    inv_den = 1.0 / (ex[0] + ex[1] + ex[2] + ex[3])

    sel16 = selected.astype(BF16)
    earlier = lax.broadcasted_iota(I32, (t, t), 1) < lax.broadcasted_iota(I32, (t, t), 0)
    before = _dot(jnp.where(earlier, 1.0, 0.0).astype(BF16), sel16)
    count = jnp.sum(selected, axis=0, keepdims=True)
    lower_expert = lax.broadcasted_iota(I32, (LANES, LANES), 0) < lax.broadcasted_iota(I32, (LANES, LANES), 1)
    offset = _dot(jnp.broadcast_to(count, (8, LANES)).astype(BF16),
                  jnp.where(lower_expert, 1.0, 0.0).astype(BF16))[0:1, :]
    where_to = before + offset
    pos = jnp.zeros((t, LANES), F32)
    topg = jnp.zeros((t, LANES), F32)
    for kk in range(TOP_K):
        pk = jnp.sum(jnp.where(lane_f == top_idx[kk], where_to, 0.0), axis=-1, keepdims=True)
        pos = jnp.where(lane == kk, pk, pos)
        topg = jnp.where(lane == kk, ex[kk] * inv_den, topg)
    pos_ref[...] = pos.astype(I32)
    topg_ref[...] = topg
    cnt_ref[...] = jnp.broadcast_to(count, cnt_ref.shape)

    pos_rows = pos.T
    slot_i = lax.broadcasted_iota(I32, (t * TOP_K, t), 0).astype(F32)
    perm = jnp.zeros((t * TOP_K, t), F32)
    for kk in range(TOP_K):
        perm = jnp.where(slot_i == pos_rows[kk:kk + 1, :], 1.0, perm)
    f_sorted = _dot(perm.astype(BF16), f_hi)
    for j in range(d // LANES):
        f_ref[pl.ds(j, t * TOP_K, stride=d // LANES), :] = f_sorted[:, j * LANES:(j + 1) * LANES]


def _post_mixer(x2, og, ods, lses, kv, seq, g1, wgate, bgate, wbg, wbd, wmix, g2, wxq, wxo, g3, wr, br):
    wr_hi = wr.astype(BF16)
    wr_lo = (wr - wr_hi.astype(F32)).astype(BF16)
    m, d = x2.shape
    t = POST_TILE
    nt = seq // t
    sub = d // LANES
    row = lambda n: pl.BlockSpec((t, n), lambda i: (i, 0))
    phase = lambda a: pl.BlockSpec((None, a.shape[1], t // a.shape[1], a.shape[3]),
                                   lambda i: (i // nt, 0, i % nt, 0))
    kv_spec = pl.BlockSpec((None,) + kv.shape[1:], lambda i: (i // nt, 0, 0))
    weights = [g1, wgate, bgate, wbg, wbd, wmix, g2, wxq, wxo, g3, wr_hi, wr_lo, br]
    return pl.pallas_call(
        _post_mixer_kernel,
        grid=(m // t,),
        in_specs=[row(d), row(GLA_V)] + [phase(a) for a in ods] + [phase(a) for a in lses] + [kv_spec]
                 + [_resident(w.shape) for w in weights],
        out_specs=[row(d), pl.BlockSpec((t * TOP_K * sub, LANES), lambda i: (i, 0)), row(LANES), row(LANES),
                   pl.BlockSpec((8, LANES), lambda i: (i, 0))],
        out_shape=[jax.ShapeDtypeStruct((m, d), F32), jax.ShapeDtypeStruct((m * TOP_K * sub, LANES), F32),
                   jax.ShapeDtypeStruct((m, LANES), I32), jax.ShapeDtypeStruct((m, LANES), F32),
                   jax.ShapeDtypeStruct((m // t * 8, LANES), F32)],
        scratch_shapes=[pltpu.VMEM((DIL_HEADS, t, DIL_HEAD_DIM), F32), pltpu.VMEM((DIL_HEADS, t, DIL_HEAD_DIM), F32),
                        pltpu.VMEM((t, LANES), F32), pltpu.VMEM((t, LANES), F32)],
        compiler_params=_params("parallel"),
        name="post_mixer",
    )(x2, og, *ods, *lses, kv, *weights)


ROW_SUB = 8
RUN_BITS = tuple(1 << b for b in range(8, -1, -1))
RUN_SMALL = 64


def _wait_tiles(hbm, vmem, sem):
    pltpu.make_async_copy(hbm.at[pl.ds(0, vmem.shape[0]), :], vmem, sem).wait()


def _experts_kernel(be_ref, nu_ref, r0_ref, nv_ref, jlo_ref, jhi_ref, cum_ref, off_ref,
                    f_hbm, wg_ref, bg_ref, wu_ref, bu_ref, wd_ref, bd_ref, y_hbm,
                    xbuf, ybuf, in_sems, out_sems, *, n_tiles, tile_rows):
    i = pl.program_id(0)
    n_used = nu_ref[0]
    rows = xbuf.shape[1] // ROW_SUB
    spare_row = n_tiles * tile_rows

    def copy_rows(hbm_row, buf_row, count, start_piece):
        def piece(bit):
            @pl.when((count & bit) != 0)
            def _():
                taken = count & (-2 * bit)
                start_piece(hbm_row + taken, buf_row + taken, bit)

        @pl.when(count >= RUN_SMALL)
        def _():
            for bit in RUN_BITS:
                if RUN_SMALL <= bit <= rows:
                    piece(bit)
        for bit in RUN_BITS:
            if bit < RUN_SMALL:
                piece(bit)

    def for_runs(b, start_piece, filler_row):
        e, r0, nv = be_ref[b], r0_ref[b], nv_ref[b]

        def body(j, carry):
            lo, hi = cum_ref[e * (n_tiles + 1) + j], cum_ref[e * (n_tiles + 1) + j + 1]
            first = jnp.maximum(lo, r0)
            count = jnp.maximum(jnp.minimum(hi, r0 + nv) - first, 0)
            copy_rows(j * tile_rows + off_ref[j * N_EXPERTS + e] + first - lo, first - r0, count, start_piece)
            return carry
        lax.fori_loop(jlo_ref[b], jhi_ref[b], body, 0)

        @pl.when(nv < rows)
        def _():
            copy_rows(filler_row, nv, rows - nv, start_piece)

    def gather(b, s):
        def piece(hbm_row, buf_row, n):
            pltpu.make_async_copy(f_hbm.at[pl.ds(hbm_row * ROW_SUB, n * ROW_SUB), :],
                                  xbuf.at[s, pl.ds(buf_row * ROW_SUB, n * ROW_SUB), :], in_sems.at[s]).start()
        for_runs(b, piece, 0)

    def scatter(b, s):
        def piece(hbm_row, buf_row, n):
            pltpu.make_async_copy(ybuf.at[s, pl.ds(buf_row * ROW_SUB, n * ROW_SUB), :],
                                  y_hbm.at[pl.ds(hbm_row * ROW_SUB, n * ROW_SUB), :], out_sems.at[s]).start()
        for_runs(b, piece, spare_row + s * rows)

    xs, ys = lax.rem(i, 2), lax.rem(i, 3)

    @pl.when(i == 0)
    def _():
        ybuf[2] = jnp.zeros(ybuf.shape[1:], F32)
        spare = [pltpu.make_async_copy(ybuf.at[2], y_hbm.at[pl.ds((spare_row + s * rows) * ROW_SUB, rows * ROW_SUB), :],
                                       out_sems.at[2]) for s in range(3)]
        for cp in spare:
            cp.start()
        for cp in spare:
            cp.wait()

    @pl.when((i >= 3) & (i - 3 < n_used))
    def _():
        _wait_tiles(y_hbm, ybuf.at[ys], out_sems.at[ys])

    @pl.when((i == 0) & (n_used > 0))
    def _():
        gather(0, 0)

    @pl.when(i + 1 < n_used)
    def _():
        gather(i + 1, lax.rem(i + 1, 2))

    @pl.when((i >= 1) & (i - 1 < n_used))
    def _():
        scatter(i - 1, lax.rem(i + 2, 3))

    @pl.when(i < n_used)
    def _():
        _wait_tiles(f_hbm, xbuf.at[xs], in_sems.at[xs])
        xb = jnp.concatenate([xbuf[xs, pl.ds(j, rows, stride=ROW_SUB), :] for j in range(ROW_SUB)],
                             axis=-1).astype(BF16)
        glu = jnp.minimum(_dot(xb, wg_ref[...]) + bg_ref[...], SWIGLU_LIMIT)
        lin = jnp.clip(_dot(xb, wu_ref[...]) + bu_ref[...], -SWIGLU_LIMIT, SWIGLU_LIMIT)
        act = glu * jax.nn.sigmoid(SWIGLU_ALPHA * glu) * (lin + 1.0)
        y = _dot(act.astype(BF16), wd_ref[...]) + bd_ref[...]
        for j in range(ROW_SUB):
            ybuf[ys, pl.ds(j, rows, stride=ROW_SUB), :] = y[:, j * LANES:(j + 1) * LANES]


def _experts(block_e, n_used, block_r0, block_nv, block_jlo, block_jhi, cum_flat, off_flat,
             f, wg, bg, wu, bu, wd, bd, n_tiles, tile_rows):
    rows = MOE_ROWS
    n_blocks = block_e.shape[0]
    wspec = lambda w: pl.BlockSpec((None,) + w.shape[1:],
                                   lambda i, be, *_: (be[jnp.minimum(i, n_blocks - 1)], 0, 0))
    return pl.pallas_call(
        functools.partial(_experts_kernel, n_tiles=n_tiles, tile_rows=tile_rows),
        grid_spec=pltpu.PrefetchScalarGridSpec(
            num_scalar_prefetch=8,
            grid=(n_blocks + 3,),
            in_specs=[pl.BlockSpec(memory_space=pl.ANY),
                      wspec(wg), wspec(bg), wspec(wu), wspec(bu), wspec(wd), wspec(bd)],
            out_specs=pl.BlockSpec(memory_space=pl.ANY),
            scratch_shapes=[pltpu.VMEM((2, rows * ROW_SUB, LANES), F32), pltpu.VMEM((3, rows * ROW_SUB, LANES), F32),
                            pltpu.SemaphoreType.DMA((2,)), pltpu.SemaphoreType.DMA((3,))],
        ),
        out_shape=jax.ShapeDtypeStruct(((n_tiles * tile_rows + 3 * rows) * ROW_SUB, LANES), F32),
        compiler_params=_params("arbitrary"),
        name="experts",
    )(block_e, n_used, block_r0, block_nv, block_jlo, block_jhi, cum_flat, off_flat, f, wg, bg, wu, bu, wd, bd)


def _combine_kernel(y_ref, h_ref, pos_ref, topg_ref, g_ref, o_ref, *, final):
    t, d = h_ref.shape
    n_sorted = t * TOP_K
    pos = pos_ref[...]
    gate = topg_ref[...]
    slot_i = lax.broadcasted_iota(I32, (t, n_sorted), 1)
    w = jnp.zeros((t, n_sorted), F32)
    for kk in range(TOP_K):
        w = jnp.where(slot_i == pos[:, kk:kk + 1], gate[:, kk:kk + 1], w)
    y = jnp.concatenate([y_ref[pl.ds(j, n_sorted, stride=ROW_SUB), :] for j in range(ROW_SUB)], axis=-1)
    acc = h_ref[...] + _dot(w.astype(BF16), y.astype(BF16))
    o_ref[...] = _rms(acc, g_ref[...]) if final else acc


def _combine(y, h, pos, topg, g, final):
    m, d = h.shape
    t = POST_TILE
    row = lambda n: pl.BlockSpec((t, n), lambda i: (i, 0))
    return pl.pallas_call(
        functools.partial(_combine_kernel, final=final),
        grid=(m // t,),
        in_specs=[pl.BlockSpec((t * TOP_K * ROW_SUB, LANES), lambda i: (i, 0)), row(d), row(LANES), row(LANES),
                  pl.BlockSpec(g.shape, lambda i: (0, 0))],
        out_specs=row(d),
        out_shape=jax.ShapeDtypeStruct((m, d), F32),
        compiler_params=_params("parallel"),
        name="combine",
    )(y, h, pos, topg, g)


def _pad_lanes(w, axis):
    pad = [(0, 0)] * w.ndim
    pad[axis] = (0, LANES - w.shape[axis])
    return jnp.pad(w, pad)


def kernel(x, mem, positions, norm1_g, w_in, w_gla_a2, b_gla_a2, gla_norm_g, w_branch_gla, w_branch_dil, w_branch_gate, b_branch_gate, w_mix_out, norm2_g, mem_norm_g, w_xq, w_xkv, w_xo, norm3_g, w_router, b_router, w_gate, b_gate, w_up, b_up, w_down, b_down, final_norm_g):
    batch, seq, d = x.shape
    m = batch * seq
    h = x.reshape(m, d)
    pos2 = positions.reshape(m, 1)
    row = lambda v: v.reshape(1, -1)
    half = DIL_HEAD_DIM // 2
    inv_freq = ROPE_THETA ** (-jnp.arange(half, dtype=F32) / half)
    invf = jnp.concatenate([inv_freq, inv_freq]).reshape(1, DIL_HEAD_DIM)
    off_alr = 2 * GLA_QK + 2 * GLA_V
    off_q = off_alr + GLA_GATE_RANK

    for l in range(w_in.shape[0]):
        w_l = w_in[l]
        g1 = row(norm1_g[l])
        q_g, k_g, v_g, r_g, log_a = _gla_proj(
            h, g1, w_l[:, :off_alr].astype(BF16), _pad_lanes(w_l[:, off_alr:off_q], 1).astype(BF16),
            _pad_lanes(w_gla_a2[l], 0).astype(BF16), row(b_gla_a2[l]))
        dil = _dil_proj(h, pos2, g1, invf, w_l[:, off_q:off_q + DIL_W].astype(BF16),
                        w_l[:, off_q + DIL_W:off_q + 2 * DIL_W].astype(BF16),
                        w_l[:, off_q + 2 * DIL_W:].astype(BF16), batch, seq)
        o_g = _gla(q_g, k_g, v_g, r_g, log_a, row(gla_norm_g[l]), batch, seq)
        ods, lses = [], []
        for g in range(DIL_GROUPS):
            o_i, lse_i = _dil_attn(dil[g], dil[DIL_GROUPS + g], dil[2 * DIL_GROUPS + g])
            ods.append(o_i)
            lses.append(lse_i)
        kv = _mem_kv(mem, row(mem_norm_g[l]), w_xkv[l].astype(BF16))
        h2, f, pos, topg, cnt = _post_mixer(
            h, o_g, ods, lses, kv, seq, g1, w_branch_gate[l].astype(BF16), row(b_branch_gate[l]),
            w_branch_gla[l].astype(BF16), w_branch_dil[l].astype(BF16), w_mix_out[l].astype(BF16),
            row(norm2_g[l]), w_xq[l].astype(BF16), w_xo[l].astype(BF16), row(norm3_g[l]),
            _pad_lanes(w_router[l], 1), _pad_lanes(row(b_router[l]), 1))

        rows = MOE_ROWS
        n_tiles = m // POST_TILE
        tile_rows = POST_TILE * TOP_K
        count = cnt.reshape(n_tiles, 8, LANES)[:, 0, :N_EXPERTS].astype(I32)
        cum = jnp.concatenate([jnp.zeros((1, N_EXPERTS), I32), jnp.cumsum(count, axis=0)], axis=0).T
        off = jnp.cumsum(count, axis=1) - count
        total = cum[:, -1]
        padded = ((total + rows - 1) // rows) * rows
        pend = jnp.cumsum(padded)
        pstart = pend - padded
        n_blocks = -(-(m * TOP_K) // rows) + N_EXPERTS
        block_start = jnp.arange(n_blocks, dtype=I32) * rows
        block_e = jnp.minimum(jnp.sum((block_start[:, None] >= pend[None, :]).astype(I32), axis=1), N_EXPERTS - 1)
        n_used = (pend[-1:] // rows).astype(I32)
        block_r0 = block_start - pstart[block_e]
        block_nv = jnp.clip(total[block_e] - block_r0, 0, rows)
        cum_b = cum[block_e]
        block_jlo = jnp.sum((cum_b[:, 1:] <= block_r0[:, None]).astype(I32), axis=1)
        block_jhi = jnp.sum((cum_b[:, :-1] < (block_r0 + block_nv)[:, None]).astype(I32), axis=1)
        y = _experts(block_e, n_used, block_r0, block_nv, block_jlo, block_jhi, cum.reshape(-1), off.reshape(-1),
                     f, w_gate[l].astype(BF16), b_gate[l][:, None, :], w_up[l].astype(BF16), b_up[l][:, None, :],
                     w_down[l].astype(BF16), b_down[l][:, None, :], n_tiles, tile_rows)
        h = _combine(y, h2, pos, topg, row(final_norm_g), final=l == w_in.shape[0] - 1)
    return h.reshape(batch, seq, d)
```

```python
import functools

import numpy as np
import jax
import jax.numpy as jnp
from jax import lax
from jax.experimental import pallas as pl
from jax.experimental.pallas import tpu as pltpu

F32 = jnp.float32
BF16 = jnp.bfloat16
I32 = jnp.int32

EPS = 1e-5
ROPE_THETA = 10000.0
GLA_HEADS = 4
GLA_DK = 64
GLA_DV = 128
GLA_GATE_RANK = 16
GLA_GATE_TEMP = 16.0
GLA_CHUNK = 64
GLA_QK = GLA_HEADS * GLA_DK
GLA_V = GLA_HEADS * GLA_DV
DIL_PAIRS = ((128, 1), (512, 4), (2048, 16))
DIL_GROUPS = 3
DIL_HEADS = 4
DIL_HEAD_DIM = 128
DIL_BLOCK = 128
DIL_GW = DIL_HEADS * DIL_HEAD_DIM
DIL_W = DIL_GROUPS * DIL_GW
XATTN_HEADS = 4
XATTN_HEAD_DIM = 128
N_EXPERTS = 32
TOP_K = 4
SWIGLU_ALPHA = 1.702
SWIGLU_LIMIT = 7.0

LANES = 128
NEG_BIG = -1e30
VMEM_LIMIT = 56 * 1024 * 1024

TOKEN_TILE = 512
POST_TILE = 256
GLA_TILE = 512
MOE_ROWS = 512
MERGE_COLS = 256


def _params(*sem):
    return pltpu.CompilerParams(dimension_semantics=sem, vmem_limit_bytes=VMEM_LIMIT)


def _resident(shape):
    nd = len(shape)
    return pl.BlockSpec(shape, lambda *_: (0,) * nd, pipeline_mode=pl.Buffered(1))


def _rms(x, g):
    return x * lax.rsqrt(jnp.mean(x * x, axis=-1, keepdims=True) + EPS) * g


def _dot(a, b):
    return jnp.dot(a, b, preferred_element_type=F32)


def _dot_nt(a, b):
    return lax.dot_general(a, b, (((1,), (1,)), ((), ())), preferred_element_type=F32)


def _gla_proj_kernel(x_ref, g_ref, w_ref, walr_ref, wa2_ref, ba2_ref,
                     q_ref, k_ref, v_ref, r_ref, la_ref):
    a = _rms(x_ref[...], g_ref[...]).astype(BF16)
    y = _dot(a, w_ref[...])
    q_ref[...] = (y[:, :GLA_QK] * (GLA_DK ** -0.5)).astype(BF16)
    k_ref[...] = y[:, GLA_QK:2 * GLA_QK].astype(BF16)
    v_ref[...] = y[:, 2 * GLA_QK:2 * GLA_QK + GLA_V].astype(BF16)
    r = y[:, 2 * GLA_QK + GLA_V:]
    r_ref[...] = (r * jax.nn.sigmoid(r)).astype(BF16)
    a_lr = _dot(a, walr_ref[...])
    logit = _dot(a_lr.astype(BF16), wa2_ref[...]) + ba2_ref[...]
    log_sig = jnp.minimum(logit, 0.0) - jnp.log1p(jnp.exp(-jnp.abs(logit)))
    la_ref[...] = log_sig * (1.0 / GLA_GATE_TEMP)


def _gla_proj(x2, g, w, walr, wa2, ba2):
    m, d = x2.shape
    t = TOKEN_TILE
    row = lambda n: pl.BlockSpec((t, n), lambda i: (i, 0))
    return pl.pallas_call(
        _gla_proj_kernel,
        grid=(m // t,),
        in_specs=[row(d), _resident(g.shape), _resident(w.shape), _resident(walr.shape),
                  _resident(wa2.shape), _resident(ba2.shape)],
        out_specs=[row(GLA_QK), row(GLA_QK), row(GLA_V), row(GLA_V), row(GLA_QK)],
        out_shape=[jax.ShapeDtypeStruct((m, GLA_QK), BF16), jax.ShapeDtypeStruct((m, GLA_QK), BF16),
                   jax.ShapeDtypeStruct((m, GLA_V), BF16), jax.ShapeDtypeStruct((m, GLA_V), BF16),
                   jax.ShapeDtypeStruct((m, GLA_QK), F32)],
        compiler_params=_params("parallel"),
        name="gla_proj",
    )(x2, g, w, walr, wa2, ba2)


def _dil_proj_kernel(x_ref, pos_ref, g_ref, invf_ref, wq_ref, wk_ref, wv_ref, *refs):
    q_refs, k_refs, v_refs = refs[0:3], refs[3:6], refs[6:9]
    stage = refs[9:]
    a = _rms(x_ref[...], g_ref[...]).astype(BF16)
    t = a.shape[0]
    ang = pos_ref[...].astype(F32) * invf_ref[...]
    lane = lax.broadcasted_iota(I32, ang.shape, 1)
    first_half = lane < DIL_HEAD_DIM // 2
    cos_sin = jnp.cos(jnp.where(first_half, ang, ang - (np.pi / 2)))
    sin_cos = pltpu.roll(cos_sin, DIL_HEAD_DIM // 2, 1)
    cos = jnp.where(first_half, cos_sin, sin_cos)
    sin_signed = jnp.where(first_half, -sin_cos, cos_sin)

    def rope(v):
        return v * cos + pltpu.roll(v, DIL_HEAD_DIM // 2, 1) * sin_signed

    post = (lambda v: rope(v) * (DIL_HEAD_DIM ** -0.5), rope, lambda v: v)
    pair = 2 * DIL_HEAD_DIM
    for ti, (w_ref, out_refs) in enumerate(((wq_ref, q_refs), (wk_ref, k_refs), (wv_ref, v_refs))):
        for g, (_, dil) in enumerate(DIL_PAIRS):
            for h2 in range(DIL_HEADS // 2):
                y = _dot(a, w_ref[:, g * DIL_GW + h2 * pair:g * DIL_GW + (h2 + 1) * pair])
                for h in (2 * h2, 2 * h2 + 1):
                    hs = slice(h * DIL_HEAD_DIM, (h + 1) * DIL_HEAD_DIM)
                    val = post[ti](y[:, (h - 2 * h2) * DIL_HEAD_DIM:(h - 2 * h2 + 1) * DIL_HEAD_DIM])
                    if dil == 1:
                        out_refs[g][0, :, hs] = val.astype(BF16)
                    else:
                        buf = stage[ti * (DIL_GROUPS - 1) + g - 1]
                        buf[h] = val
                        for p in range(dil):
                            out_refs[g][p, :, hs] = buf[h, pl.ds(p, t // dil, stride=dil), :].astype(BF16)


def _dil_proj(x2, pos2, g, invf, wq, wk, wv, batch, seq):
    m, d = x2.shape
    t = TOKEN_TILE
    nt = seq // t
    row = lambda n: pl.BlockSpec((t, n), lambda i: (i, 0))
    out_specs, out_shape = [], []
    for _ in range(3):
        for _, dil in DIL_PAIRS:
            out_specs.append(pl.BlockSpec((None, dil, t // dil, DIL_GW), lambda i: (i // nt, 0, i % nt, 0)))
            out_shape.append(jax.ShapeDtypeStruct((batch, dil, seq // dil, DIL_GW), BF16))
    return pl.pallas_call(
        _dil_proj_kernel,
        grid=(m // t,),
        in_specs=[row(d), row(1), _resident(g.shape), _resident(invf.shape),
                  _resident(wq.shape), _resident(wk.shape), _resident(wv.shape)],
        out_specs=out_specs,
        out_shape=out_shape,
        scratch_shapes=[pltpu.VMEM((DIL_HEADS, t, DIL_HEAD_DIM), F32)] * (3 * (DIL_GROUPS - 1)),
        compiler_params=_params("parallel"),
        name="dil_proj",
    )(x2, pos2, g, invf, wq, wk, wv)


GLA_LEVELS = (32, 16, 8, 4, 2, 1)
GLA_ONE_FACTOR_MIN = -60.0


def _gla_level_masks():
    c = GLA_CHUNK
    t = np.arange(c)[:, None]
    r = np.arange(c)[None, :]
    blocks = []
    for h in GLA_LEVELS:
        u = t % (2 * h)
        m = t - u + h - 1
        blocks.append(((u >= h) & (r > m) & (r <= t)).astype(np.float32))
        blocks.append(((u < h) & (r > t) & (r <= m)).astype(np.float32))
    return np.concatenate(blocks, axis=0)


def _gla_kernel(q_ref, k_ref, v_ref, r_ref, la_ref, gn_ref, tri_ref, lvl_ref, o_ref, st_ref, bc_ref):
    @pl.when(pl.program_id(1) == 0)
    def _():
        st_ref[...] = jnp.zeros_like(st_ref)

    c = GLA_CHUNK
    n_chunks = q_ref.shape[0] // c
    row_i = lax.broadcasted_iota(I32, (c, c), 0)
    col_i = lax.broadcasted_iota(I32, (c, c), 1)
    gn = gn_ref[...]

    tri = tri_ref[...]
    min_last = None
    for ci in range(n_chunks):
        rows = slice(ci * c, (ci + 1) * c)
        bcum = jnp.dot(tri, la_ref[rows, :], precision=lax.Precision.HIGHEST,
                       preferred_element_type=F32)
        bc_ref[rows, :] = bcum
        last = bcum[c - 1:c, :]
        min_last = last if min_last is None else jnp.minimum(min_last, last)
    one_factor_ok = jnp.min(min_last) >= GLA_ONE_FACTOR_MIN

    def finish_chunk(rows, q_dec, k_dec, e_last, v, r, scores_of_head):
        for h in range(GLA_HEADS):
            ks = slice(h * GLA_DK, (h + 1) * GLA_DK)
            vs = slice(h * GLA_DV, (h + 1) * GLA_DV)
            vh = v[:, vs]
            st = st_ref[h]
            o = _dot(scores_of_head(h, ks).astype(BF16), vh) + _dot_nt(q_dec[:, ks], st.astype(BF16))
            v_t = vh.astype(F32).T.astype(BF16)
            st_ref[h] = st * e_last[:, ks] + _dot(v_t, k_dec[:, ks])
            o = o * lax.rsqrt(jnp.mean(o * o, axis=-1, keepdims=True) + EPS) * gn[:, vs]
            o_ref[rows, vs] = (o * r[:, vs]).astype(BF16)

    def load_chunk(rows):
        q = q_ref[rows, :].astype(F32)
        k = k_ref[rows, :].astype(F32)
        bcum = bc_ref[rows, :]
        blast = bcum[c - 1:c, :]
        q_dec = (q * jnp.exp(bcum)).astype(BF16)
        k_dec = (k * jnp.exp(blast - bcum)).astype(BF16)
        return q, k, bcum, q_dec, k_dec, jnp.exp(blast), v_ref[rows, :], r_ref[rows, :].astype(F32)

    @pl.when(one_factor_ok)
    def _():
        for ci in range(n_chunks):
            rows = slice(ci * c, (ci + 1) * c)
            q, k, bcum, q_dec, k_dec, e_last, v, r = load_chunk(rows)
            k_inv = (k * jnp.exp(-bcum)).astype(BF16)
            finish_chunk(rows, q_dec, k_dec, e_last, v, r,
                         lambda h, ks: jnp.where(row_i >= col_i, _dot_nt(q_dec[:, ks], k_inv[:, ks]), 0.0))

    @pl.when(jnp.logical_not(one_factor_ok))
    def _():
        def chunk(ci, carry):
            rows = pl.ds(pl.multiple_of(ci * c, c), c)
            q, k, bcum, q_dec, k_dec, e_last, v, r = load_chunk(rows)
            expo = jnp.dot(lvl_ref[...], la_ref[rows, :], precision=lax.Precision.HIGHEST,
                           preferred_element_type=F32)
            q_lv = [(q * jnp.exp(expo[2 * j * c:(2 * j + 1) * c, :])).astype(BF16) for j in range(len(GLA_LEVELS))]
            k_lv = [(k * jnp.exp(expo[(2 * j + 1) * c:(2 * j + 2) * c, :])).astype(BF16) for j in range(len(GLA_LEVELS))]
            q_bf, k_bf = q.astype(BF16), k.astype(BF16)

            def scores_of_head(h, ks):
                sc = jnp.where(row_i == col_i, _dot_nt(q_bf[:, ks], k_bf[:, ks]), 0.0)
                differ = row_i ^ col_i
                for j, half in enumerate(GLA_LEVELS):
                    valid = (differ >= half) & (differ < 2 * half) & ((row_i & half) != 0)
                    sc = sc + jnp.where(valid, _dot_nt(q_lv[j][:, ks], k_lv[j][:, ks]), 0.0)
                return sc
            finish_chunk(rows, q_dec, k_dec, e_last, v, r, scores_of_head)
            return carry
        lax.fori_loop(0, n_chunks, chunk, 0)


def _gla(q, k, v, r, la, gn, batch, seq):
    t = GLA_TILE
    nt = seq // t
    row = lambda n: pl.BlockSpec((t, n), lambda b, j: (b * nt + j, 0))
    const = lambda a: pl.BlockSpec(a.shape, lambda b, j: (0, 0))
    tri = jnp.tril(jnp.ones((GLA_CHUNK, GLA_CHUNK), F32))
    lvl = jnp.asarray(_gla_level_masks())
    return pl.pallas_call(
        _gla_kernel,
        grid=(batch, nt),
        in_specs=[row(GLA_QK), row(GLA_QK), row(GLA_V), row(GLA_V), row(GLA_QK), const(gn), const(tri), const(lvl)],
        out_specs=row(GLA_V),
        out_shape=jax.ShapeDtypeStruct((batch * seq, GLA_V), BF16),
        scratch_shapes=[pltpu.VMEM((GLA_HEADS, GLA_DV, GLA_DK), F32), pltpu.VMEM((t, GLA_QK), F32)],
        compiler_params=_params("parallel", "arbitrary"),
        name="gla",
    )(q, k, v, r, la, gn, tri, lvl)


def _dil_attn_kernel(q_ref, k_ref, v_ref, o_ref, lse_ref, kp_ref, vp_ref):
    n = pl.program_id(2)

    @pl.when(n == 0)
    def _():
        kp_ref[...] = jnp.zeros_like(kp_ref)
        vp_ref[...] = jnp.zeros_like(vp_ref)

    qb = DIL_BLOCK
    n_blk = q_ref.shape[0] // qb
    row = lax.broadcasted_iota(I32, (qb, qb), 0)
    col = lax.broadcasted_iota(I32, (qb, qb), 1)
    lower = col <= row
    prev_bias = jnp.where(n > 0, 0.0, NEG_BIG)
    lane = lax.broadcasted_iota(I32, (qb, LANES), 1)
    blk = lambda j: slice(j * qb, (j + 1) * qb)
    lse_all = [jnp.zeros((qb, LANES), F32) for _ in range(n_blk)]
    for h in range(DIL_HEADS):
        hs = slice(h * DIL_HEAD_DIM, (h + 1) * DIL_HEAD_DIM)
        s_prev = [_dot_nt(q_ref[blk(0), hs], kp_ref[:, hs])] + [None] * (n_blk - 1)
        s_cur = [None] * n_blk
        for j in range(n_blk):
            last = min(j + 2, n_blk)
            s = _dot_nt(q_ref[j * qb:last * qb, hs], k_ref[blk(j), hs])
            s_cur[j] = s[:qb]
            if j + 1 < n_blk:
                s_prev[j + 1] = s[qb:]
        p_lower, p_upper, p_far, denom = [], [], [], []
        for j in range(n_blk):
            bias = prev_bias if j == 0 else 0.0
            s = jnp.where(lower, s_cur[j], s_prev[j] + bias)
            s_far = jnp.sum(jnp.where(row == col, s_prev[j], 0.0), axis=-1, keepdims=True) + bias
            m = jnp.maximum(jnp.max(s, axis=-1, keepdims=True), s_far)
            p = jnp.exp(s - m)
            pf = jnp.exp(s_far - m)
            l = jnp.sum(p, axis=-1, keepdims=True) + pf
            p_lower.append(jnp.where(lower, p, 0.0).astype(BF16))
            p_upper.append(jnp.where(lower, 0.0, p).astype(BF16))
            p_far.append(pf)
            denom.append(l)
            lse_all[j] = jnp.where(lane == h, m + jnp.log(l), lse_all[j])
        vp = vp_ref[:, hs]
        acc = [_dot(p_upper[0], vp) + p_far[0] * vp.astype(F32)]
        acc += [p_far[j] * v_ref[blk(j - 1), hs].astype(F32) for j in range(1, n_blk)]
        for j in range(n_blk):
            if j + 1 < n_blk:
                o2 = _dot(jnp.concatenate([p_lower[j], p_upper[j + 1]], axis=0), v_ref[blk(j), hs])
                acc[j] = acc[j] + o2[:qb]
                acc[j + 1] = acc[j + 1] + o2[qb:]
            else:
                acc[j] = acc[j] + _dot(p_lower[j], v_ref[blk(j), hs])
        for j in range(n_blk):
            o_ref[blk(j), hs] = (acc[j] / denom[j]).astype(BF16)
    for j in range(n_blk):
        lse_ref[blk(j), :] = lse_all[j]
    kp_ref[...] = k_ref[blk(n_blk - 1), :]
    vp_ref[...] = v_ref[blk(n_blk - 1), :]


DIL_BLOCKS_PER_STEP = 4


def _dil_attn(q, k, v):
    batch, dilation, length, _ = q.shape
    n_blk = max(g for g in range(1, DIL_BLOCKS_PER_STEP + 1) if (length // DIL_BLOCK) % g == 0)
    rows = n_blk * DIL_BLOCK
    blk = lambda n: pl.BlockSpec((None, None, rows, n), lambda b, p, i: (b, p, i, 0))
    return pl.pallas_call(
        _dil_attn_kernel,
        grid=(batch, dilation, length // rows),
        in_specs=[blk(DIL_GW)] * 3,
        out_specs=[blk(DIL_GW), blk(LANES)],
        out_shape=[jax.ShapeDtypeStruct(q.shape, BF16),
                   jax.ShapeDtypeStruct((batch, dilation, length, LANES), F32)],
        scratch_shapes=[pltpu.VMEM((DIL_BLOCK, DIL_GW), BF16), pltpu.VMEM((DIL_BLOCK, DIL_GW), BF16)],
        compiler_params=_params("parallel", "parallel", "arbitrary"),
        name=f"dil_attn_d{dilation}",
    )(q, k, v)


def _mem_kv_kernel(mem_ref, g_ref, w_ref, kv_ref):
    kv_ref[...] = _dot(_rms(mem_ref[...], g_ref[...]).astype(BF16), w_ref[...]).astype(BF16)


def _mem_kv(mem, g, w):
    b, ml, d = mem.shape
    n = w.shape[1]
    return pl.pallas_call(
        _mem_kv_kernel,
        grid=(b,),
        in_specs=[pl.BlockSpec((None, ml, d), lambda i: (i, 0, 0)), _resident(g.shape), _resident(w.shape)],
        out_specs=pl.BlockSpec((None, ml, n), lambda i: (i, 0, 0)),
        out_shape=jax.ShapeDtypeStruct((b, ml, n), BF16),
        compiler_params=_params("parallel"),
        name="mem_kv",
    )(mem, g, w)


def _post_mixer_kernel(x_ref, og_ref, od0_ref, od1_ref, od2_ref, l0_ref, l1_ref, l2_ref, kv_ref,
                       g1_ref, wgate_ref, bgate_ref, wbg_ref, wbd_ref, wmix_ref,
                       g2_ref, wxq_ref, wxo_ref, g3_ref, wrh_ref, wrl_ref, br_ref,
                       h_ref, f_ref, pos_ref, topg_ref, cnt_ref,
                       o1_buf, o2_buf, l1_buf, l2_buf):
    x = x_ref[...]
    t, d = x.shape
    a = _rms(x, g1_ref[...]).astype(BF16)

    for od_ref, l_ref, o_buf, l_buf in ((od1_ref, l1_ref, o1_buf, l1_buf), (od2_ref, l2_ref, o2_buf, l2_buf)):
        dil = od_ref.shape[0]
        for p in range(dil):
            rows = pl.ds(p, t // dil, stride=dil)
            l_buf[rows, :] = l_ref[p]
            for h in range(DIL_HEADS):
                o_buf[h, rows, :] = od_ref[p, :, h * DIL_HEAD_DIM:(h + 1) * DIL_HEAD_DIM].astype(F32)

    l0, l1, l2 = l0_ref[0], l1_buf[...], l2_buf[...]
    lm = jnp.maximum(jnp.maximum(l0, l1), l2)
    e0, e1, e2 = jnp.exp(l0 - lm), jnp.exp(l1 - lm), jnp.exp(l2 - lm)
    inv = 1.0 / (e0 + e1 + e2)
    w0, w1, w2 = e0 * inv, e1 * inv, e2 * inv
    od_heads = []
    for h in range(DIL_HEADS):
        hs = slice(h * DIL_HEAD_DIM, (h + 1) * DIL_HEAD_DIM)
        od_heads.append(w0[:, h:h + 1] * od0_ref[0, :, hs].astype(F32)
                        + w1[:, h:h + 1] * o1_buf[h] + w2[:, h:h + 1] * o2_buf[h])
    o_d = jnp.concatenate(od_heads, axis=-1).astype(BF16)

    og = og_ref[...]
    merged = []
    for c0 in range(0, d, MERGE_COLS):
        cs, cs_dil = slice(c0, c0 + MERGE_COLS), slice(d + c0, d + c0 + MERGE_COLS)
        gate_gla = jax.nn.sigmoid(_dot(a, wgate_ref[:, cs]) + bgate_ref[:, cs])
        gate_dil = jax.nn.sigmoid(_dot(a, wgate_ref[:, cs_dil]) + bgate_ref[:, cs_dil])
        merged.append((gate_gla * _dot(og, wbg_ref[:, cs]) + gate_dil * _dot(o_d, wbd_ref[:, cs])).astype(BF16))
    h1 = x + _dot(jnp.concatenate(merged, axis=-1), wmix_ref[...])

    c = _rms(h1, g2_ref[...]).astype(BF16)
    qx = (_dot(c, wxq_ref[...]) * (XATTN_HEAD_DIM ** -0.5)).astype(BF16)
    xw = XATTN_HEADS * XATTN_HEAD_DIM
    heads = []
    for h in range(XATTN_HEADS):
        hs = slice(h * XATTN_HEAD_DIM, (h + 1) * XATTN_HEAD_DIM)
        vs = slice(xw + h * XATTN_HEAD_DIM, xw + (h + 1) * XATTN_HEAD_DIM)
        s = _dot_nt(qx[:, hs], kv_ref[:, hs])
        p = jnp.exp(s - jnp.max(s, axis=-1, keepdims=True))
        o = _dot(p.astype(BF16), kv_ref[:, vs]) / jnp.sum(p, axis=-1, keepdims=True)
        heads.append(o.astype(BF16))
    h2 = h1 + _dot(jnp.concatenate(heads, axis=-1), wxo_ref[...])
    h_ref[...] = h2

    f = _rms(h2, g3_ref[...])
    f_hi = f.astype(BF16)
    f_lo = (f - f_hi.astype(F32)).astype(BF16)
    logits = (_dot(f_hi, wrh_ref[...]) + (_dot(f_lo, wrh_ref[...]) + _dot(f_hi, wrl_ref[...]))) + br_ref[...]
    lane = lax.broadcasted_iota(I32, (t, LANES), 1)
    work = jnp.where(lane < N_EXPERTS, logits, NEG_BIG)
    selected = jnp.zeros((t, LANES), F32)
    top_val, top_idx = [], []
    lane_f = lane.astype(F32)
    for _ in range(TOP_K):
        mx = jnp.max(work, axis=-1, keepdims=True)
        idx = jnp.min(jnp.where(work == mx, lane_f, float(LANES)), axis=-1, keepdims=True)
        hit = lane_f == idx
        selected = jnp.where(hit, 1.0, selected)
        work = jnp.where(hit, NEG_BIG, work)
        top_val.append(mx)
        top_idx.append(idx)
    ex = [jnp.exp(v - top_val[0]) for v in top_val]
    inv_den = 1.0 / (ex[0] + ex[1] + ex[2] + ex[3])

    sel16 = selected.astype(BF16)
    earlier = lax.broadcasted_iota(I32, (t, t), 1) < lax.broadcasted_iota(I32, (t, t), 0)
    before = _dot(jnp.where(earlier, 1.0, 0.0).astype(BF16), sel16)
    count = jnp.sum(selected, axis=0, keepdims=True)
    lower_expert = lax.broadcasted_iota(I32, (LANES, LANES), 0) < lax.broadcasted_iota(I32, (LANES, LANES), 1)
    offset = _dot(jnp.broadcast_to(count, (8, LANES)).astype(BF16),
                  jnp.where(lower_expert, 1.0, 0.0).astype(BF16))[0:1, :]
    where_to = before + offset
    pos = jnp.zeros((t, LANES), F32)
    topg = jnp.zeros((t, LANES), F32)
    for kk in range(TOP_K):
        pk = jnp.sum(jnp.where(lane_f == top_idx[kk], where_to, 0.0), axis=-1, keepdims=True)
        pos = jnp.where(lane == kk, pk, pos)
        topg = jnp.where(lane == kk, ex[kk] * inv_den, topg)
    pos_ref[...] = pos.astype(I32)
    topg_ref[...] = topg
    cnt_ref[...] = jnp.broadcast_to(count, cnt_ref.shape)

    pos_rows = pos.T
    slot_i = lax.broadcasted_iota(I32, (t * TOP_K, t), 0).astype(F32)
    perm = jnp.zeros((t * TOP_K, t), F32)
    for kk in range(TOP_K):
        perm = jnp.where(slot_i == pos_rows[kk:kk + 1, :], 1.0, perm)
    f_sorted = _dot(perm.astype(BF16), f_hi)
    for j in range(d // LANES):
        f_ref[pl.ds(j, t * TOP_K, stride=d // LANES), :] = f_sorted[:, j * LANES:(j + 1) * LANES]


def _post_mixer(x2, og, ods, lses, kv, seq, g1, wgate, bgate, wbg, wbd, wmix, g2, wxq, wxo, g3, wr, br):
    wr_hi = wr.astype(BF16)
    wr_lo = (wr - wr_hi.astype(F32)).astype(BF16)
    m, d = x2.shape
    t = POST_TILE
    nt = seq // t
    sub = d // LANES
    row = lambda n: pl.BlockSpec((t, n), lambda i: (i, 0))
    phase = lambda a: pl.BlockSpec((None, a.shape[1], t // a.shape[1], a.shape[3]),
                                   lambda i: (i // nt, 0, i % nt, 0))
    kv_spec = pl.BlockSpec((None,) + kv.shape[1:], lambda i: (i // nt, 0, 0))
    weights = [g1, wgate, bgate, wbg, wbd, wmix, g2, wxq, wxo, g3, wr_hi, wr_lo, br]
    return pl.pallas_call(
        _post_mixer_kernel,
        grid=(m // t,),
        in_specs=[row(d), row(GLA_V)] + [phase(a) for a in ods] + [phase(a) for a in lses] + [kv_spec]
                 + [_resident(w.shape) for w in weights],
        out_specs=[row(d), pl.BlockSpec((t * TOP_K * sub, LANES), lambda i: (i, 0)), row(LANES), row(LANES),
                   pl.BlockSpec((8, LANES), lambda i: (i, 0))],
        out_shape=[jax.ShapeDtypeStruct((m, d), F32), jax.ShapeDtypeStruct((m * TOP_K * sub, LANES), F32),
                   jax.ShapeDtypeStruct((m, LANES), I32), jax.ShapeDtypeStruct((m, LANES), F32),
                   jax.ShapeDtypeStruct((m // t * 8, LANES), F32)],
        scratch_shapes=[pltpu.VMEM((DIL_HEADS, t, DIL_HEAD_DIM), F32), pltpu.VMEM((DIL_HEADS, t, DIL_HEAD_DIM), F32),
                        pltpu.VMEM((t, LANES), F32), pltpu.VMEM((t, LANES), F32)],
        compiler_params=_params("parallel"),
        name="post_mixer",
    )(x2, og, *ods, *lses, kv, *weights)


ROW_SUB = 8
RUN_BITS = tuple(1 << b for b in range(MOE_ROWS.bit_length() - 1, -1, -1))
RUN_SMALL = 64


def _wait_tiles(hbm, vmem, sem):
    pltpu.make_async_copy(hbm.at[pl.ds(0, vmem.shape[0]), :], vmem, sem).wait()


def _experts_kernel(be_ref, nu_ref, r0_ref, nv_ref, jlo_ref, jhi_ref, cum_ref, off_ref,
                    f_hbm, wg_ref, bg_ref, wu_ref, bu_ref, wd_ref, bd_ref, y_hbm,
                    xbuf, ybuf, in_sems, out_sems, *, n_tiles, tile_rows):
    i = pl.program_id(0)
    n_used = nu_ref[0]
    rows = xbuf.shape[1] // ROW_SUB
    spare_row = n_tiles * tile_rows

    def copy_rows(hbm_row, buf_row, count, start_piece):
        def piece(bit):
            @pl.when((count & bit) != 0)
            def _():
                taken = count & (-2 * bit)
                start_piece(hbm_row + taken, buf_row + taken, bit)

        @pl.when(count >= RUN_SMALL)
        def _():
            for bit in RUN_BITS:
                if RUN_SMALL <= bit <= rows:
                    piece(bit)
        for bit in RUN_BITS:
            if bit < RUN_SMALL:
                piece(bit)

    def for_runs(b, start_piece, filler_row):
        e, r0, nv = be_ref[b], r0_ref[b], nv_ref[b]

        def body(j, carry):
            lo, hi = cum_ref[e * (n_tiles + 1) + j], cum_ref[e * (n_tiles + 1) + j + 1]
            first = jnp.maximum(lo, r0)
            count = jnp.maximum(jnp.minimum(hi, r0 + nv) - first, 0)
            copy_rows(j * tile_rows + off_ref[j * N_EXPERTS + e] + first - lo, first - r0, count, start_piece)
            return carry
        lax.fori_loop(jlo_ref[b], jhi_ref[b], body, 0)

        @pl.when(nv < rows)
        def _():
            copy_rows(filler_row, nv, rows - nv, start_piece)

    def gather(b, s):
        def piece(hbm_row, buf_row, n):
            pltpu.make_async_copy(f_hbm.at[pl.ds(hbm_row * ROW_SUB, n * ROW_SUB), :],
                                  xbuf.at[s, pl.ds(buf_row * ROW_SUB, n * ROW_SUB), :], in_sems.at[s]).start()
        for_runs(b, piece, 0)

    def scatter(b, s):
        def piece(hbm_row, buf_row, n):
            pltpu.make_async_copy(ybuf.at[s, pl.ds(buf_row * ROW_SUB, n * ROW_SUB), :],
                                  y_hbm.at[pl.ds(hbm_row * ROW_SUB, n * ROW_SUB), :], out_sems.at[s]).start()
        for_runs(b, piece, spare_row + s * rows)

    xs, ys = lax.rem(i, 2), lax.rem(i, 3)

    @pl.when(i == 0)
    def _():
        ybuf[2] = jnp.zeros(ybuf.shape[1:], F32)
        spare = [pltpu.make_async_copy(ybuf.at[2], y_hbm.at[pl.ds((spare_row + s * rows) * ROW_SUB, rows * ROW_SUB), :],
                                       out_sems.at[2]) for s in range(3)]
        for cp in spare:
            cp.start()
        for cp in spare:
            cp.wait()

    @pl.when((i >= 3) & (i - 3 < n_used))
    def _():
        _wait_tiles(y_hbm, ybuf.at[ys], out_sems.at[ys])

    @pl.when((i == 0) & (n_used > 0))
    def _():
        gather(0, 0)

    @pl.when(i + 1 < n_used)
    def _():
        gather(i + 1, lax.rem(i + 1, 2))

    @pl.when((i >= 1) & (i - 1 < n_used))
    def _():
        scatter(i - 1, lax.rem(i + 2, 3))

    @pl.when(i < n_used)
    def _():
        _wait_tiles(f_hbm, xbuf.at[xs], in_sems.at[xs])
        xb = jnp.concatenate([xbuf[xs, pl.ds(j, rows, stride=ROW_SUB), :] for j in range(ROW_SUB)],
                             axis=-1).astype(BF16)
        glu = jnp.minimum(_dot(xb, wg_ref[...]) + bg_ref[...], SWIGLU_LIMIT)
        lin = jnp.clip(_dot(xb, wu_ref[...]) + bu_ref[...], -SWIGLU_LIMIT, SWIGLU_LIMIT)
        act = glu * jax.nn.sigmoid(SWIGLU_ALPHA * glu) * (lin + 1.0)
        y = _dot(act.astype(BF16), wd_ref[...]) + bd_ref[...]
        for j in range(ROW_SUB):
            ybuf[ys, pl.ds(j, rows, stride=ROW_SUB), :] = y[:, j * LANES:(j + 1) * LANES]


def _experts(block_e, n_used, block_r0, block_nv, block_jlo, block_jhi, cum_flat, off_flat,
             f, wg, bg, wu, bu, wd, bd, n_tiles, tile_rows):
    rows = MOE_ROWS
    n_blocks = block_e.shape[0]
    wspec = lambda w: pl.BlockSpec((None,) + w.shape[1:],
                                   lambda i, be, *_: (be[jnp.minimum(i, n_blocks - 1)], 0, 0))
    return pl.pallas_call(
        functools.partial(_experts_kernel, n_tiles=n_tiles, tile_rows=tile_rows),
        grid_spec=pltpu.PrefetchScalarGridSpec(
            num_scalar_prefetch=8,
            grid=(n_blocks + 3,),
            in_specs=[pl.BlockSpec(memory_space=pl.ANY),
                      wspec(wg), wspec(bg), wspec(wu), wspec(bu), wspec(wd), wspec(bd)],
            out_specs=pl.BlockSpec(memory_space=pl.ANY),
            scratch_shapes=[pltpu.VMEM((2, rows * ROW_SUB, LANES), F32), pltpu.VMEM((3, rows * ROW_SUB, LANES), F32),
                            pltpu.SemaphoreType.DMA((2,)), pltpu.SemaphoreType.DMA((3,))],
        ),
        out_shape=jax.ShapeDtypeStruct(((n_tiles * tile_rows + 3 * rows) * ROW_SUB, LANES), F32),
        compiler_params=_params("arbitrary"),
        name="experts",
    )(block_e, n_used, block_r0, block_nv, block_jlo, block_jhi, cum_flat, off_flat, f, wg, bg, wu, bu, wd, bd)


def _combine_kernel(y_ref, h_ref, pos_ref, topg_ref, g_ref, o_ref, *, final):
    t, d = h_ref.shape
    n_sorted = t * TOP_K
    pos = pos_ref[...]
    gate = topg_ref[...]
    slot_i = lax.broadcasted_iota(I32, (t, n_sorted), 1)
    w = jnp.zeros((t, n_sorted), F32)
    for kk in range(TOP_K):
        w = jnp.where(slot_i == pos[:, kk:kk + 1], gate[:, kk:kk + 1], w)
    y = jnp.concatenate([y_ref[pl.ds(j, n_sorted, stride=ROW_SUB), :] for j in range(ROW_SUB)], axis=-1)
    acc = h_ref[...] + _dot(w.astype(BF16), y.astype(BF16))
    o_ref[...] = _rms(acc, g_ref[...]) if final else acc


def _combine(y, h, pos, topg, g, final):
    m, d = h.shape
    t = POST_TILE
    row = lambda n: pl.BlockSpec((t, n), lambda i: (i, 0))
    return pl.pallas_call(
        functools.partial(_combine_kernel, final=final),
        grid=(m // t,),
        in_specs=[pl.BlockSpec((t * TOP_K * ROW_SUB, LANES), lambda i: (i, 0)), row(d), row(LANES), row(LANES),
                  pl.BlockSpec(g.shape, lambda i: (0, 0))],
        out_specs=row(d),
        out_shape=jax.ShapeDtypeStruct((m, d), F32),
        compiler_params=_params("parallel"),
        name="combine",
    )(y, h, pos, topg, g)


def _pad_lanes(w, axis):
    pad = [(0, 0)] * w.ndim
    pad[axis] = (0, LANES - w.shape[axis])
    return jnp.pad(w, pad)


def kernel(x, mem, positions, norm1_g, w_in, w_gla_a2, b_gla_a2, gla_norm_g, w_branch_gla, w_branch_dil, w_branch_gate, b_branch_gate, w_mix_out, norm2_g, mem_norm_g, w_xq, w_xkv, w_xo, norm3_g, w_router, b_router, w_gate, b_gate, w_up, b_up, w_down, b_down, final_norm_g):
    batch, seq, d = x.shape
    m = batch * seq
    h = x.reshape(m, d)
    pos2 = positions.reshape(m, 1)
    row = lambda v: v.reshape(1, -1)
    half = DIL_HEAD_DIM // 2
    inv_freq = ROPE_THETA ** (-jnp.arange(half, dtype=F32) / half)
    invf = jnp.concatenate([inv_freq, inv_freq]).reshape(1, DIL_HEAD_DIM)
    off_alr = 2 * GLA_QK + 2 * GLA_V
    off_q = off_alr + GLA_GATE_RANK

    for l in range(w_in.shape[0]):
        w_l = w_in[l]
        g1 = row(norm1_g[l])
        q_g, k_g, v_g, r_g, log_a = _gla_proj(
            h, g1, w_l[:, :off_alr].astype(BF16), _pad_lanes(w_l[:, off_alr:off_q], 1).astype(BF16),
            _pad_lanes(w_gla_a2[l], 0).astype(BF16), row(b_gla_a2[l]))
        dil = _dil_proj(h, pos2, g1, invf, w_l[:, off_q:off_q + DIL_W].astype(BF16),
                        w_l[:, off_q + DIL_W:off_q + 2 * DIL_W].astype(BF16),
                        w_l[:, off_q + 2 * DIL_W:].astype(BF16), batch, seq)
        o_g = _gla(q_g, k_g, v_g, r_g, log_a, row(gla_norm_g[l]), batch, seq)
        ods, lses = [], []
        for g in range(DIL_GROUPS):
            o_i, lse_i = _dil_attn(dil[g], dil[DIL_GROUPS + g], dil[2 * DIL_GROUPS + g])
            ods.append(o_i)
            lses.append(lse_i)
        kv = _mem_kv(mem, row(mem_norm_g[l]), w_xkv[l].astype(BF16))
        h2, f, pos, topg, cnt = _post_mixer(
            h, o_g, ods, lses, kv, seq, g1, w_branch_gate[l].astype(BF16), row(b_branch_gate[l]),
            w_branch_gla[l].astype(BF16), w_branch_dil[l].astype(BF16), w_mix_out[l].astype(BF16),
            row(norm2_g[l]), w_xq[l].astype(BF16), w_xo[l].astype(BF16), row(norm3_g[l]),
            _pad_lanes(w_router[l], 1), _pad_lanes(row(b_router[l]), 1))

        rows = MOE_ROWS
        n_tiles = m // POST_TILE
        tile_rows = POST_TILE * TOP_K
        count = cnt.reshape(n_tiles, 8, LANES)[:, 0, :N_EXPERTS].astype(I32)
        cum = jnp.concatenate([jnp.zeros((1, N_EXPERTS), I32), jnp.cumsum(count, axis=0)], axis=0).T
        off = jnp.cumsum(count, axis=1) - count
        total = cum[:, -1]
        padded = ((total + rows - 1) // rows) * rows
        pend = jnp.cumsum(padded)
        pstart = pend - padded
        n_blocks = -(-(m * TOP_K) // rows) + N_EXPERTS
        block_start = jnp.arange(n_blocks, dtype=I32) * rows
        block_e = jnp.minimum(jnp.sum((block_start[:, None] >= pend[None, :]).astype(I32), axis=1), N_EXPERTS - 1)
        n_used = (pend[-1:] // rows).astype(I32)
        block_r0 = block_start - pstart[block_e]
        block_nv = jnp.clip(total[block_e] - block_r0, 0, rows)
        cum_b = cum[block_e]
        block_jlo = jnp.sum((cum_b[:, 1:] <= block_r0[:, None]).astype(I32), axis=1)
        block_jhi = jnp.sum((cum_b[:, :-1] < (block_r0 + block_nv)[:, None]).astype(I32), axis=1)
        y = _experts(block_e, n_used, block_r0, block_nv, block_jlo, block_jhi, cum.reshape(-1), off.reshape(-1),
                     f, w_gate[l].astype(BF16), b_gate[l][:, None, :], w_up[l].astype(BF16), b_up[l][:, None, :],
                     w_down[l].astype(BF16), b_down[l][:, None, :], n_tiles, tile_rows)
        h = _combine(y, h2, pos, topg, row(final_norm_g), final=l == w_in.shape[0] - 1)
    return h.reshape(batch, seq, d)
```

```python
import functools

import numpy as np
import jax
import jax.numpy as jnp
from jax import lax
from jax.experimental import pallas as pl
from jax.experimental.pallas import tpu as pltpu

F32 = jnp.float32
BF16 = jnp.bfloat16
I32 = jnp.int32

EPS = 1e-5
ROPE_THETA = 10000.0
GLA_HEADS = 4
GLA_DK = 64
GLA_DV = 128
GLA_GATE_RANK = 16
GLA_GATE_TEMP = 16.0
GLA_CHUNK = 64
GLA_QK = GLA_HEADS * GLA_DK
GLA_V = GLA_HEADS * GLA_DV
DIL_PAIRS = ((128, 1), (512, 4), (2048, 16))
DIL_GROUPS = 3
DIL_HEADS = 4
DIL_HEAD_DIM = 128
DIL_BLOCK = 128
DIL_GW = DIL_HEADS * DIL_HEAD_DIM
DIL_W = DIL_GROUPS * DIL_GW
XATTN_HEADS = 4
XATTN_HEAD_DIM = 128
N_EXPERTS = 32
TOP_K = 4
SWIGLU_ALPHA = 1.702
SWIGLU_LIMIT = 7.0

LANES = 128
NEG_BIG = -1e30
VMEM_LIMIT = 56 * 1024 * 1024

TOKEN_TILE = 512
POST_TILE = 256
GLA_TILE = 512
MOE_ROWS = 512
MERGE_COLS = 256


def _params(*sem):
    return pltpu.CompilerParams(dimension_semantics=sem, vmem_limit_bytes=VMEM_LIMIT)


def _resident(shape):
    nd = len(shape)
    return pl.BlockSpec(shape, lambda *_: (0,) * nd, pipeline_mode=pl.Buffered(1))


def _rms(x, g):
    return x * lax.rsqrt(jnp.mean(x * x, axis=-1, keepdims=True) + EPS) * g


def _dot(a, b):
    return jnp.dot(a, b, preferred_element_type=F32)


def _dot_nt(a, b):
    return lax.dot_general(a, b, (((1,), (1,)), ((), ())), preferred_element_type=F32)


def _gla_proj_body(a, w_ref, walr_ref, wa2_ref, ba2_ref, q_ref, k_ref, v_ref, r_ref, la_ref):
    y = _dot(a, w_ref[...])
    q_ref[...] = (y[:, :GLA_QK] * (GLA_DK ** -0.5)).astype(BF16)
    k_ref[...] = y[:, GLA_QK:2 * GLA_QK].astype(BF16)
    v_ref[...] = y[:, 2 * GLA_QK:2 * GLA_QK + GLA_V].astype(BF16)
    r = y[:, 2 * GLA_QK + GLA_V:]
    r_ref[...] = (r * jax.nn.sigmoid(r)).astype(BF16)
    a_lr = _dot(a, walr_ref[...])
    logit = _dot(a_lr.astype(BF16), wa2_ref[...]) + ba2_ref[...]
    log_sig = jnp.minimum(logit, 0.0) - jnp.log1p(jnp.exp(-jnp.abs(logit)))
    la_ref[...] = log_sig * (1.0 / GLA_GATE_TEMP)


def _dil_proj_body(a, pos_ref, invf_ref, wq_ref, wk_ref, wv_ref, refs):
    q_refs, k_refs, v_refs = refs[0:3], refs[3:6], refs[6:9]
    stage = refs[9:]
    t = a.shape[0]
    ang = pos_ref[...].astype(F32) * invf_ref[...]
    lane = lax.broadcasted_iota(I32, ang.shape, 1)
    first_half = lane < DIL_HEAD_DIM // 2
    cos_sin = jnp.cos(jnp.where(first_half, ang, ang - (np.pi / 2)))
    sin_cos = pltpu.roll(cos_sin, DIL_HEAD_DIM // 2, 1)
    cos = jnp.where(first_half, cos_sin, sin_cos)
    sin_signed = jnp.where(first_half, -sin_cos, cos_sin)

    def rope(v):
        return v * cos + pltpu.roll(v, DIL_HEAD_DIM // 2, 1) * sin_signed

    post = (lambda v: rope(v) * (DIL_HEAD_DIM ** -0.5), rope, lambda v: v)
    pair = 2 * DIL_HEAD_DIM
    for ti, (w_ref, out_refs) in enumerate(((wq_ref, q_refs), (wk_ref, k_refs), (wv_ref, v_refs))):
        for g, (_, dil) in enumerate(DIL_PAIRS):
            for h2 in range(DIL_HEADS // 2):
                y = _dot(a, w_ref[:, g * DIL_GW + h2 * pair:g * DIL_GW + (h2 + 1) * pair])
                for h in (2 * h2, 2 * h2 + 1):
                    hs = slice(h * DIL_HEAD_DIM, (h + 1) * DIL_HEAD_DIM)
                    val = post[ti](y[:, (h - 2 * h2) * DIL_HEAD_DIM:(h - 2 * h2 + 1) * DIL_HEAD_DIM])
                    if dil == 1:
                        out_refs[g][0, :, hs] = val.astype(BF16)
                    else:
                        buf = stage[ti * (DIL_GROUPS - 1) + g - 1]
                        buf[h] = val
                        for p in range(dil):
                            out_refs[g][p, :, hs] = buf[h, pl.ds(p, t // dil, stride=dil), :].astype(BF16)


N_GLA_OUT = 5


def _in_proj_kernel(x_ref, pos_ref, g_ref, invf_ref, w_ref, walr_ref, wa2_ref, ba2_ref, wq_ref, wk_ref, wv_ref,
                    *refs):
    a = _rms(x_ref[...], g_ref[...]).astype(BF16)
    _gla_proj_body(a, w_ref, walr_ref, wa2_ref, ba2_ref, *refs[:N_GLA_OUT])
    _dil_proj_body(a, pos_ref, invf_ref, wq_ref, wk_ref, wv_ref, refs[N_GLA_OUT:])


def _in_proj(x2, pos2, g, invf, w, walr, wa2, ba2, wq, wk, wv, batch, seq):
    m, d = x2.shape
    t = TOKEN_TILE
    nt = seq // t
    row = lambda n: pl.BlockSpec((t, n), lambda i: (i, 0))
    out_specs = [row(GLA_QK), row(GLA_QK), row(GLA_V), row(GLA_V), row(GLA_QK)]
    out_shape = [jax.ShapeDtypeStruct((m, GLA_QK), BF16), jax.ShapeDtypeStruct((m, GLA_QK), BF16),
                 jax.ShapeDtypeStruct((m, GLA_V), BF16), jax.ShapeDtypeStruct((m, GLA_V), BF16),
                 jax.ShapeDtypeStruct((m, GLA_QK), F32)]
    for _ in range(3):
        for _, dil in DIL_PAIRS:
            out_specs.append(pl.BlockSpec((None, dil, t // dil, DIL_GW), lambda i: (i // nt, 0, i % nt, 0)))
            out_shape.append(jax.ShapeDtypeStruct((batch, dil, seq // dil, DIL_GW), BF16))
    weights = [g, invf, w, walr, wa2, ba2, wq, wk, wv]
    outs = pl.pallas_call(
        _in_proj_kernel,
        grid=(m // t,),
        in_specs=[row(d), row(1)] + [_resident(a.shape) for a in weights],
        out_specs=out_specs,
        out_shape=out_shape,
        scratch_shapes=[pltpu.VMEM((DIL_HEADS, t, DIL_HEAD_DIM), F32)] * (3 * (DIL_GROUPS - 1)),
        compiler_params=_params("parallel"),
        name="in_proj",
    )(x2, pos2, *weights)
    return outs[:N_GLA_OUT], outs[N_GLA_OUT:]


GLA_LEVELS = (32, 16, 8, 4, 2, 1)
GLA_ONE_FACTOR_MIN = -60.0


def _gla_level_masks():
    c = GLA_CHUNK
    t = np.arange(c)[:, None]
    r = np.arange(c)[None, :]
    blocks = []
    for h in GLA_LEVELS:
        u = t % (2 * h)
        m = t - u + h - 1
        blocks.append(((u >= h) & (r > m) & (r <= t)).astype(np.float32))
        blocks.append(((u < h) & (r > t) & (r <= m)).astype(np.float32))
    return np.concatenate(blocks, axis=0)


def _gla_kernel(q_ref, k_ref, v_ref, r_ref, la_ref, gn_ref, tri_ref, lvl_ref, o_ref, st_ref, bc_ref):
    @pl.when(pl.program_id(1) == 0)
    def _():
        st_ref[...] = jnp.zeros_like(st_ref)

    c = GLA_CHUNK
    n_chunks = q_ref.shape[0] // c
    row_i = lax.broadcasted_iota(I32, (c, c), 0)
    col_i = lax.broadcasted_iota(I32, (c, c), 1)
    gn = gn_ref[...]

    tri = tri_ref[...]
    min_last = None
    for ci in range(n_chunks):
        rows = slice(ci * c, (ci + 1) * c)
        bcum = jnp.dot(tri, la_ref[rows, :], precision=lax.Precision.HIGHEST,
                       preferred_element_type=F32)
        bc_ref[rows, :] = bcum
        last = bcum[c - 1:c, :]
        min_last = last if min_last is None else jnp.minimum(min_last, last)
    one_factor_ok = jnp.min(min_last) >= GLA_ONE_FACTOR_MIN

    def finish_chunk(rows, q_dec, k_dec, e_last, v, r, scores_of_head):
        for h in range(GLA_HEADS):
            ks = slice(h * GLA_DK, (h + 1) * GLA_DK)
            vs = slice(h * GLA_DV, (h + 1) * GLA_DV)
            vh = v[:, vs]
            st = st_ref[h]
            o = _dot(scores_of_head(h, ks).astype(BF16), vh) + _dot_nt(q_dec[:, ks], st.astype(BF16))
            v_t = vh.astype(F32).T.astype(BF16)
            st_ref[h] = st * e_last[:, ks] + _dot(v_t, k_dec[:, ks])
            o = o * lax.rsqrt(jnp.mean(o * o, axis=-1, keepdims=True) + EPS) * gn[:, vs]
            o_ref[rows, vs] = (o * r[:, vs]).astype(BF16)

    def load_chunk(rows):
        q = q_ref[rows, :].astype(F32)
        k = k_ref[rows, :].astype(F32)
        bcum = bc_ref[rows, :]
        blast = bcum[c - 1:c, :]
        q_dec = (q * jnp.exp(bcum)).astype(BF16)
        k_dec = (k * jnp.exp(blast - bcum)).astype(BF16)
        return q, k, bcum, q_dec, k_dec, jnp.exp(blast), v_ref[rows, :], r_ref[rows, :].astype(F32)

    @pl.when(one_factor_ok)
    def _():
        for ci in range(n_chunks):
            rows = slice(ci * c, (ci + 1) * c)
            q, k, bcum, q_dec, k_dec, e_last, v, r = load_chunk(rows)
            k_inv = (k * jnp.exp(-bcum)).astype(BF16)
            finish_chunk(rows, q_dec, k_dec, e_last, v, r,
                         lambda h, ks: jnp.where(row_i >= col_i, _dot_nt(q_dec[:, ks], k_inv[:, ks]), 0.0))

    @pl.when(jnp.logical_not(one_factor_ok))
    def _():
        def chunk(ci, carry):
            rows = pl.ds(pl.multiple_of(ci * c, c), c)
            q, k, bcum, q_dec, k_dec, e_last, v, r = load_chunk(rows)
            expo = jnp.dot(lvl_ref[...], la_ref[rows, :], precision=lax.Precision.HIGHEST,
                           preferred_element_type=F32)
            q_lv = [(q * jnp.exp(expo[2 * j * c:(2 * j + 1) * c, :])).astype(BF16) for j in range(len(GLA_LEVELS))]
            k_lv = [(k * jnp.exp(expo[(2 * j + 1) * c:(2 * j + 2) * c, :])).astype(BF16) for j in range(len(GLA_LEVELS))]
            q_bf, k_bf = q.astype(BF16), k.astype(BF16)

            def scores_of_head(h, ks):
                sc = jnp.where(row_i == col_i, _dot_nt(q_bf[:, ks], k_bf[:, ks]), 0.0)
                differ = row_i ^ col_i
                for j, half in enumerate(GLA_LEVELS):
                    valid = (differ >= half) & (differ < 2 * half) & ((row_i & half) != 0)
                    sc = sc + jnp.where(valid, _dot_nt(q_lv[j][:, ks], k_lv[j][:, ks]), 0.0)
                return sc
            finish_chunk(rows, q_dec, k_dec, e_last, v, r, scores_of_head)
            return carry
        lax.fori_loop(0, n_chunks, chunk, 0)


def _gla(q, k, v, r, la, gn, batch, seq):
    t = GLA_TILE
    nt = seq // t
    row = lambda n: pl.BlockSpec((t, n), lambda b, j: (b * nt + j, 0))
    const = lambda a: pl.BlockSpec(a.shape, lambda b, j: (0, 0))
    tri = jnp.tril(jnp.ones((GLA_CHUNK, GLA_CHUNK), F32))
    lvl = jnp.asarray(_gla_level_masks())
    return pl.pallas_call(
        _gla_kernel,
        grid=(batch, nt),
        in_specs=[row(GLA_QK), row(GLA_QK), row(GLA_V), row(GLA_V), row(GLA_QK), const(gn), const(tri), const(lvl)],
        out_specs=row(GLA_V),
        out_shape=jax.ShapeDtypeStruct((batch * seq, GLA_V), BF16),
        scratch_shapes=[pltpu.VMEM((GLA_HEADS, GLA_DV, GLA_DK), F32), pltpu.VMEM((t, GLA_QK), F32)],
        compiler_params=_params("parallel", "arbitrary"),
        name="gla",
    )(q, k, v, r, la, gn, tri, lvl)


def _dil_attn_kernel(q_ref, k_ref, v_ref, o_ref, lse_ref, kp_ref, vp_ref):
    n = pl.program_id(2)

    @pl.when(n == 0)
    def _():
        kp_ref[...] = jnp.zeros_like(kp_ref)
        vp_ref[...] = jnp.zeros_like(vp_ref)

    qb = DIL_BLOCK
    n_blk = q_ref.shape[0] // qb
    row = lax.broadcasted_iota(I32, (qb, qb), 0)
    col = lax.broadcasted_iota(I32, (qb, qb), 1)
    lower = col <= row
    prev_bias = jnp.where(n > 0, 0.0, NEG_BIG)
    lane = lax.broadcasted_iota(I32, (qb, LANES), 1)
    blk = lambda j: slice(j * qb, (j + 1) * qb)
    lse_all = [jnp.zeros((qb, LANES), F32) for _ in range(n_blk)]
    for h in range(DIL_HEADS):
        hs = slice(h * DIL_HEAD_DIM, (h + 1) * DIL_HEAD_DIM)
        s_prev = [_dot_nt(q_ref[blk(0), hs], kp_ref[:, hs])] + [None] * (n_blk - 1)
        s_cur = [None] * n_blk
        for j in range(n_blk):
            last = min(j + 2, n_blk)
            s = _dot_nt(q_ref[j * qb:last * qb, hs], k_ref[blk(j), hs])
            s_cur[j] = s[:qb]
            if j + 1 < n_blk:
                s_prev[j + 1] = s[qb:]
        p_lower, p_upper, p_far, denom = [], [], [], []
        for j in range(n_blk):
            bias = prev_bias if j == 0 else 0.0
            s = jnp.where(lower, s_cur[j], s_prev[j] + bias)
            s_far = jnp.sum(jnp.where(row == col, s_prev[j], 0.0), axis=-1, keepdims=True) + bias
            m = jnp.maximum(jnp.max(s, axis=-1, keepdims=True), s_far)
            p = jnp.exp(s - m)
            pf = jnp.exp(s_far - m)
            l = jnp.sum(p, axis=-1, keepdims=True) + pf
            p_lower.append(jnp.where(lower, p, 0.0).astype(BF16))
            p_upper.append(jnp.where(lower, 0.0, p).astype(BF16))
            p_far.append(pf)
            denom.append(l)
            lse_all[j] = jnp.where(lane == h, m + jnp.log(l), lse_all[j])
        vp = vp_ref[:, hs]
        acc = [_dot(p_upper[0], vp) + p_far[0] * vp.astype(F32)]
        acc += [p_far[j] * v_ref[blk(j - 1), hs].astype(F32) for j in range(1, n_blk)]
        for j in range(n_blk):
            if j + 1 < n_blk:
                o2 = _dot(jnp.concatenate([p_lower[j], p_upper[j + 1]], axis=0), v_ref[blk(j), hs])
                acc[j] = acc[j] + o2[:qb]
                acc[j + 1] = acc[j + 1] + o2[qb:]
            else:
                acc[j] = acc[j] + _dot(p_lower[j], v_ref[blk(j), hs])
        for j in range(n_blk):
            o_ref[blk(j), hs] = (acc[j] / denom[j]).astype(BF16)
    for j in range(n_blk):
        lse_ref[blk(j), :] = lse_all[j]
    kp_ref[...] = k_ref[blk(n_blk - 1), :]
    vp_ref[...] = v_ref[blk(n_blk - 1), :]


DIL_BLOCKS_PER_STEP = 4


def _dil_attn(q, k, v):
    batch, dilation, length, _ = q.shape
    n_blk = max(g for g in range(1, DIL_BLOCKS_PER_STEP + 1) if (length // DIL_BLOCK) % g == 0)
    rows = n_blk * DIL_BLOCK
    blk = lambda n: pl.BlockSpec((None, None, rows, n), lambda b, p, i: (b, p, i, 0))
    return pl.pallas_call(
        _dil_attn_kernel,
        grid=(batch, dilation, length // rows),
        in_specs=[blk(DIL_GW)] * 3,
        out_specs=[blk(DIL_GW), blk(LANES)],
        out_shape=[jax.ShapeDtypeStruct(q.shape, BF16),
                   jax.ShapeDtypeStruct((batch, dilation, length, LANES), F32)],
        scratch_shapes=[pltpu.VMEM((DIL_BLOCK, DIL_GW), BF16), pltpu.VMEM((DIL_BLOCK, DIL_GW), BF16)],
        compiler_params=_params("parallel", "parallel", "arbitrary"),
        name=f"dil_attn_d{dilation}",
    )(q, k, v)


def _mem_kv_kernel(mem_ref, g_ref, w_ref, kv_ref):
    kv_ref[...] = _dot(_rms(mem_ref[...], g_ref[...]).astype(BF16), w_ref[...]).astype(BF16)


def _mem_kv(mem, g, w):
    b, ml, d = mem.shape
    n = w.shape[1]
    return pl.pallas_call(
        _mem_kv_kernel,
        grid=(b,),
        in_specs=[pl.BlockSpec((None, ml, d), lambda i: (i, 0, 0)), _resident(g.shape), _resident(w.shape)],
        out_specs=pl.BlockSpec((None, ml, n), lambda i: (i, 0, 0)),
        out_shape=jax.ShapeDtypeStruct((b, ml, n), BF16),
        compiler_params=_params("parallel"),
        name="mem_kv",
    )(mem, g, w)


def _post_mixer_kernel(x_ref, og_ref, od0_ref, od1_ref, od2_ref, l0_ref, l1_ref, l2_ref, kv_ref,
                       g1_ref, wgate_ref, bgate_ref, wbg_ref, wbd_ref, wmix_ref,
                       g2_ref, wxq_ref, wxo_ref, g3_ref, wrh_ref, wrl_ref, br_ref,
                       h_ref, f_ref, pos_ref, topg_ref, cnt_ref,
                       o1_buf, o2_buf, l1_buf, l2_buf):
    x = x_ref[...]
    t, d = x.shape
    a = _rms(x, g1_ref[...]).astype(BF16)

    for od_ref, l_ref, o_buf, l_buf in ((od1_ref, l1_ref, o1_buf, l1_buf), (od2_ref, l2_ref, o2_buf, l2_buf)):
        dil = od_ref.shape[0]
        for p in range(dil):
            rows = pl.ds(p, t // dil, stride=dil)
            l_buf[rows, :] = l_ref[p]
            for h in range(DIL_HEADS):
                o_buf[h, rows, :] = od_ref[p, :, h * DIL_HEAD_DIM:(h + 1) * DIL_HEAD_DIM].astype(F32)

    l0, l1, l2 = l0_ref[0], l1_buf[...], l2_buf[...]
    lm = jnp.maximum(jnp.maximum(l0, l1), l2)
    e0, e1, e2 = jnp.exp(l0 - lm), jnp.exp(l1 - lm), jnp.exp(l2 - lm)
    inv = 1.0 / (e0 + e1 + e2)
    w0, w1, w2 = e0 * inv, e1 * inv, e2 * inv
    od_heads = []
    for h in range(DIL_HEADS):
        hs = slice(h * DIL_HEAD_DIM, (h + 1) * DIL_HEAD_DIM)
        od_heads.append(w0[:, h:h + 1] * od0_ref[0, :, hs].astype(F32)
                        + w1[:, h:h + 1] * o1_buf[h] + w2[:, h:h + 1] * o2_buf[h])
    o_d = jnp.concatenate(od_heads, axis=-1).astype(BF16)

    og = og_ref[...]
    merged = []
    for c0 in range(0, d, MERGE_COLS):
        cs, cs_dil = slice(c0, c0 + MERGE_COLS), slice(d + c0, d + c0 + MERGE_COLS)
        gate_gla = jax.nn.sigmoid(_dot(a, wgate_ref[:, cs]) + bgate_ref[:, cs])
        gate_dil = jax.nn.sigmoid(_dot(a, wgate_ref[:, cs_dil]) + bgate_ref[:, cs_dil])
        merged.append((gate_gla * _dot(og, wbg_ref[:, cs]) + gate_dil * _dot(o_d, wbd_ref[:, cs])).astype(BF16))
    h1 = x + _dot(jnp.concatenate(merged, axis=-1), wmix_ref[...])

    c = _rms(h1, g2_ref[...]).astype(BF16)
    qx = (_dot(c, wxq_ref[...]) * (XATTN_HEAD_DIM ** -0.5)).astype(BF16)
    xw = XATTN_HEADS * XATTN_HEAD_DIM
    heads = []
    for h in range(XATTN_HEADS):
        hs = slice(h * XATTN_HEAD_DIM, (h + 1) * XATTN_HEAD_DIM)
        vs = slice(xw + h * XATTN_HEAD_DIM, xw + (h + 1) * XATTN_HEAD_DIM)
        s = _dot_nt(qx[:, hs], kv_ref[:, hs])
        p = jnp.exp(s - jnp.max(s, axis=-1, keepdims=True))
        o = _dot(p.astype(BF16), kv_ref[:, vs]) / jnp.sum(p, axis=-1, keepdims=True)
        heads.append(o.astype(BF16))
    h2 = h1 + _dot(jnp.concatenate(heads, axis=-1), wxo_ref[...])
    h_ref[...] = h2

    f = _rms(h2, g3_ref[...])
    f_hi = f.astype(BF16)
    f_lo = (f - f_hi.astype(F32)).astype(BF16)
    logits = (_dot(f_hi, wrh_ref[...]) + (_dot(f_lo, wrh_ref[...]) + _dot(f_hi, wrl_ref[...]))) + br_ref[...]
    lane = lax.broadcasted_iota(I32, (t, LANES), 1)
    work = jnp.where(lane < N_EXPERTS, logits, NEG_BIG)
    selected = jnp.zeros((t, LANES), F32)
    top_val, top_idx = [], []
    lane_f = lane.astype(F32)
    for _ in range(TOP_K):
        mx = jnp.max(work, axis=-1, keepdims=True)
        idx = jnp.min(jnp.where(work == mx, lane_f, float(LANES)), axis=-1, keepdims=True)
        hit = lane_f == idx
        selected = jnp.where(hit, 1.0, selected)
        work = jnp.where(hit, NEG_BIG, work)
        top_val.append(mx)
        top_idx.append(idx)
    ex = [jnp.exp(v - top_val[0]) for v in top_val]
    inv_den = 1.0 / (ex[0] + ex[1] + ex[2] + ex[3])

    sel16 = selected.astype(BF16)
    earlier = lax.broadcasted_iota(I32, (t, t), 1) < lax.broadcasted_iota(I32, (t, t), 0)
    before = _dot(jnp.where(earlier, 1.0, 0.0).astype(BF16), sel16)
    count = jnp.sum(selected, axis=0, keepdims=True)
    lower_expert = lax.broadcasted_iota(I32, (LANES, LANES), 0) < lax.broadcasted_iota(I32, (LANES, LANES), 1)
    offset = _dot(jnp.broadcast_to(count, (8, LANES)).astype(BF16),
                  jnp.where(lower_expert, 1.0, 0.0).astype(BF16))[0:1, :]
    where_to = before + offset
    pos = jnp.zeros((t, LANES), F32)
    topg = jnp.zeros((t, LANES), F32)
    for kk in range(TOP_K):
        pk = jnp.sum(jnp.where(lane_f == top_idx[kk], where_to, 0.0), axis=-1, keepdims=True)
        pos = jnp.where(lane == kk, pk, pos)
        topg = jnp.where(lane == kk, ex[kk] * inv_den, topg)
    pos_ref[...] = pos.astype(I32)
    topg_ref[...] = topg
    cnt_ref[...] = jnp.broadcast_to(count, cnt_ref.shape)

    pos_rows = pos.T
    slot_i = lax.broadcasted_iota(I32, (t * TOP_K, t), 0).astype(F32)
    perm = jnp.zeros((t * TOP_K, t), F32)
    for kk in range(TOP_K):
        perm = jnp.where(slot_i == pos_rows[kk:kk + 1, :], 1.0, perm)
    f_sorted = _dot(perm.astype(BF16), f_hi)
    for j in range(d // LANES):
        f_ref[pl.ds(j, t * TOP_K, stride=d // LANES), :] = f_sorted[:, j * LANES:(j + 1) * LANES]


def _post_mixer(x2, og, ods, lses, kv, seq, g1, wgate, bgate, wbg, wbd, wmix, g2, wxq, wxo, g3, wr, br):
    wr_hi = wr.astype(BF16)
    wr_lo = (wr - wr_hi.astype(F32)).astype(BF16)
    m, d = x2.shape
    t = POST_TILE
    nt = seq // t
    sub = d // LANES
    row = lambda n: pl.BlockSpec((t, n), lambda i: (i, 0))
    phase = lambda a: pl.BlockSpec((None, a.shape[1], t // a.shape[1], a.shape[3]),
                                   lambda i: (i // nt, 0, i % nt, 0))
    kv_spec = pl.BlockSpec((None,) + kv.shape[1:], lambda i: (i // nt, 0, 0))
    weights = [g1, wgate, bgate, wbg, wbd, wmix, g2, wxq, wxo, g3, wr_hi, wr_lo, br]
    return pl.pallas_call(
        _post_mixer_kernel,
        grid=(m // t,),
        in_specs=[row(d), row(GLA_V)] + [phase(a) for a in ods] + [phase(a) for a in lses] + [kv_spec]
                 + [_resident(w.shape) for w in weights],
        out_specs=[row(d), pl.BlockSpec((t * TOP_K * sub, LANES), lambda i: (i, 0)), row(LANES), row(LANES),
                   pl.BlockSpec((8, LANES), lambda i: (i, 0))],
        out_shape=[jax.ShapeDtypeStruct((m, d), F32), jax.ShapeDtypeStruct((m * TOP_K * sub, LANES), F32),
                   jax.ShapeDtypeStruct((m, LANES), I32), jax.ShapeDtypeStruct((m, LANES), F32),
                   jax.ShapeDtypeStruct((m // t * 8, LANES), F32)],
        scratch_shapes=[pltpu.VMEM((DIL_HEADS, t, DIL_HEAD_DIM), F32), pltpu.VMEM((DIL_HEADS, t, DIL_HEAD_DIM), F32),
                        pltpu.VMEM((t, LANES), F32), pltpu.VMEM((t, LANES), F32)],
        compiler_params=_params("parallel"),
        name="post_mixer",
    )(x2, og, *ods, *lses, kv, *weights)


ROW_SUB = 8
RUN_BITS = tuple(1 << b for b in range(MOE_ROWS.bit_length() - 1, -1, -1))
RUN_SMALL = 64


def _wait_tiles(hbm, vmem, sem):
    pltpu.make_async_copy(hbm.at[pl.ds(0, vmem.shape[0]), :], vmem, sem).wait()


def _experts_kernel(be_ref, nu_ref, r0_ref, nv_ref, jlo_ref, jhi_ref, cum_ref, off_ref,
                    f_hbm, wg_ref, bg_ref, wu_ref, bu_ref, wd_ref, bd_ref, y_hbm,
                    xbuf, ybuf, in_sems, out_sems, *, n_tiles, tile_rows):
    i = pl.program_id(0)
    n_used = nu_ref[0]
    rows = xbuf.shape[1] // ROW_SUB
    spare_row = n_tiles * tile_rows

    def copy_rows(hbm_row, buf_row, count, start_piece):
        def piece(bit):
            @pl.when((count & bit) != 0)
            def _():
                taken = count & (-2 * bit)
                start_piece(hbm_row + taken, buf_row + taken, bit)

        @pl.when(count >= RUN_SMALL)
        def _():
            for bit in RUN_BITS:
                if RUN_SMALL <= bit <= rows:
                    piece(bit)
        for bit in RUN_BITS:
            if bit < RUN_SMALL:
                piece(bit)

    def for_runs(b, start_piece, filler_row):
        e, r0, nv = be_ref[b], r0_ref[b], nv_ref[b]

        def body(j, carry):
            lo, hi = cum_ref[e * (n_tiles + 1) + j], cum_ref[e * (n_tiles + 1) + j + 1]
            first = jnp.maximum(lo, r0)
            count = jnp.maximum(jnp.minimum(hi, r0 + nv) - first, 0)
            copy_rows(j * tile_rows + off_ref[j * N_EXPERTS + e] + first - lo, first - r0, count, start_piece)
            return carry
        lax.fori_loop(jlo_ref[b], jhi_ref[b], body, 0)

        @pl.when(nv < rows)
        def _():
            copy_rows(filler_row, nv, rows - nv, start_piece)

    def gather(b, s):
        def piece(hbm_row, buf_row, n):
            pltpu.make_async_copy(f_hbm.at[pl.ds(hbm_row * ROW_SUB, n * ROW_SUB), :],
                                  xbuf.at[s, pl.ds(buf_row * ROW_SUB, n * ROW_SUB), :], in_sems.at[s]).start()
        for_runs(b, piece, 0)

    def scatter(b, s):
        def piece(hbm_row, buf_row, n):
            pltpu.make_async_copy(ybuf.at[s, pl.ds(buf_row * ROW_SUB, n * ROW_SUB), :],
                                  y_hbm.at[pl.ds(hbm_row * ROW_SUB, n * ROW_SUB), :], out_sems.at[s]).start()
        for_runs(b, piece, spare_row + s * rows)

    xs, ys = lax.rem(i, 2), lax.rem(i, 3)

    @pl.when(i == 0)
    def _():
        ybuf[2] = jnp.zeros(ybuf.shape[1:], F32)
        spare = [pltpu.make_async_copy(ybuf.at[2], y_hbm.at[pl.ds((spare_row + s * rows) * ROW_SUB, rows * ROW_SUB), :],
                                       out_sems.at[2]) for s in range(3)]
        for cp in spare:
            cp.start()
        for cp in spare:
            cp.wait()

    @pl.when((i >= 3) & (i - 3 < n_used))
    def _():
        _wait_tiles(y_hbm, ybuf.at[ys], out_sems.at[ys])

    @pl.when((i == 0) & (n_used > 0))
    def _():
        gather(0, 0)

    @pl.when(i + 1 < n_used)
    def _():
        gather(i + 1, lax.rem(i + 1, 2))

    @pl.when((i >= 1) & (i - 1 < n_used))
    def _():
        scatter(i - 1, lax.rem(i + 2, 3))

    @pl.when(i < n_used)
    def _():
        _wait_tiles(f_hbm, xbuf.at[xs], in_sems.at[xs])
        xb = jnp.concatenate([xbuf[xs, pl.ds(j, rows, stride=ROW_SUB), :] for j in range(ROW_SUB)],
                             axis=-1).astype(BF16)
        glu = jnp.minimum(_dot(xb, wg_ref[...]) + bg_ref[...], SWIGLU_LIMIT)
        lin = jnp.clip(_dot(xb, wu_ref[...]) + bu_ref[...], -SWIGLU_LIMIT, SWIGLU_LIMIT)
        act = glu * jax.nn.sigmoid(SWIGLU_ALPHA * glu) * (lin + 1.0)
        y = _dot(act.astype(BF16), wd_ref[...]) + bd_ref[...]
        for j in range(ROW_SUB):
            ybuf[ys, pl.ds(j, rows, stride=ROW_SUB), :] = y[:, j * LANES:(j + 1) * LANES]


def _experts(block_e, n_used, block_r0, block_nv, block_jlo, block_jhi, cum_flat, off_flat,
             f, wg, bg, wu, bu, wd, bd, n_tiles, tile_rows):
    rows = MOE_ROWS
    n_blocks = block_e.shape[0]
    wspec = lambda w: pl.BlockSpec((None,) + w.shape[1:],
                                   lambda i, be, *_: (be[jnp.minimum(i, n_blocks - 1)], 0, 0))
    return pl.pallas_call(
        functools.partial(_experts_kernel, n_tiles=n_tiles, tile_rows=tile_rows),
        grid_spec=pltpu.PrefetchScalarGridSpec(
            num_scalar_prefetch=8,
            grid=(n_blocks + 3,),
            in_specs=[pl.BlockSpec(memory_space=pl.ANY),
                      wspec(wg), wspec(bg), wspec(wu), wspec(bu), wspec(wd), wspec(bd)],
            out_specs=pl.BlockSpec(memory_space=pl.ANY),
            scratch_shapes=[pltpu.VMEM((2, rows * ROW_SUB, LANES), F32), pltpu.VMEM((3, rows * ROW_SUB, LANES), F32),
                            pltpu.SemaphoreType.DMA((2,)), pltpu.SemaphoreType.DMA((3,))],
        ),
        out_shape=jax.ShapeDtypeStruct(((n_tiles * tile_rows + 3 * rows) * ROW_SUB, LANES), F32),
        compiler_params=_params("arbitrary"),
        name="experts",
    )(block_e, n_used, block_r0, block_nv, block_jlo, block_jhi, cum_flat, off_flat, f, wg, bg, wu, bu, wd, bd)


def _combine_kernel(y_ref, h_ref, pos_ref, topg_ref, g_ref, o_ref, *, final):
    t, d = h_ref.shape
    n_sorted = t * TOP_K
    pos = pos_ref[...]
    gate = topg_ref[...]
    slot_i = lax.broadcasted_iota(I32, (t, n_sorted), 1)
    w = jnp.zeros((t, n_sorted), F32)
    for kk in range(TOP_K):
        w = jnp.where(slot_i == pos[:, kk:kk + 1], gate[:, kk:kk + 1], w)
    y = jnp.concatenate([y_ref[pl.ds(j, n_sorted, stride=ROW_SUB), :] for j in range(ROW_SUB)], axis=-1)
    acc = h_ref[...] + _dot(w.astype(BF16), y.astype(BF16))
    o_ref[...] = _rms(acc, g_ref[...]) if final else acc


def _combine(y, h, pos, topg, g, final):
    m, d = h.shape
    t = POST_TILE
    row = lambda n: pl.BlockSpec((t, n), lambda i: (i, 0))
    return pl.pallas_call(
        functools.partial(_combine_kernel, final=final),
        grid=(m // t,),
        in_specs=[pl.BlockSpec((t * TOP_K * ROW_SUB, LANES), lambda i: (i, 0)), row(d), row(LANES), row(LANES),
                  pl.BlockSpec(g.shape, lambda i: (0, 0))],
        out_specs=row(d),
        out_shape=jax.ShapeDtypeStruct((m, d), F32),
        compiler_params=_params("parallel"),
        name="combine",
    )(y, h, pos, topg, g)


def _pad_lanes(w, axis):
    pad = [(0, 0)] * w.ndim
    pad[axis] = (0, LANES - w.shape[axis])
    return jnp.pad(w, pad)


def kernel(x, mem, positions, norm1_g, w_in, w_gla_a2, b_gla_a2, gla_norm_g, w_branch_gla, w_branch_dil, w_branch_gate, b_branch_gate, w_mix_out, norm2_g, mem_norm_g, w_xq, w_xkv, w_xo, norm3_g, w_router, b_router, w_gate, b_gate, w_up, b_up, w_down, b_down, final_norm_g):
    batch, seq, d = x.shape
    m = batch * seq
    h = x.reshape(m, d)
    pos2 = positions.reshape(m, 1)
    row = lambda v: v.reshape(1, -1)
    half = DIL_HEAD_DIM // 2
    inv_freq = ROPE_THETA ** (-jnp.arange(half, dtype=F32) / half)
    invf = jnp.concatenate([inv_freq, inv_freq]).reshape(1, DIL_HEAD_DIM)
    off_alr = 2 * GLA_QK + 2 * GLA_V
    off_q = off_alr + GLA_GATE_RANK

    for l in range(w_in.shape[0]):
        w_l = w_in[l]
        g1 = row(norm1_g[l])
        (q_g, k_g, v_g, r_g, log_a), dil = _in_proj(
            h, pos2, g1, invf, w_l[:, :off_alr].astype(BF16), _pad_lanes(w_l[:, off_alr:off_q], 1).astype(BF16),
            _pad_lanes(w_gla_a2[l], 0).astype(BF16), row(b_gla_a2[l]),
            w_l[:, off_q:off_q + DIL_W].astype(BF16), w_l[:, off_q + DIL_W:off_q + 2 * DIL_W].astype(BF16),
            w_l[:, off_q + 2 * DIL_W:].astype(BF16), batch, seq)
        o_g = _gla(q_g, k_g, v_g, r_g, log_a, row(gla_norm_g[l]), batch, seq)
        ods, lses = [], []
        for g in range(DIL_GROUPS):
            o_i, lse_i = _dil_attn(dil[g], dil[DIL_GROUPS + g], dil[2 * DIL_GROUPS + g])
            ods.append(o_i)
            lses.append(lse_i)
        kv = _mem_kv(mem, row(mem_norm_g[l]), w_xkv[l].astype(BF16))
        h2, f, pos, topg, cnt = _post_mixer(
            h, o_g, ods, lses, kv, seq, g1, w_branch_gate[l].astype(BF16), row(b_branch_gate[l]),
            w_branch_gla[l].astype(BF16), w_branch_dil[l].astype(BF16), w_mix_out[l].astype(BF16),
            row(norm2_g[l]), w_xq[l].astype(BF16), w_xo[l].astype(BF16), row(norm3_g[l]),
            _pad_lanes(w_router[l], 1), _pad_lanes(row(b_router[l]), 1))

        rows = MOE_ROWS
        n_tiles = m // POST_TILE
        tile_rows = POST_TILE * TOP_K
        count = cnt.reshape(n_tiles, 8, LANES)[:, 0, :N_EXPERTS].astype(I32)
        cum = jnp.concatenate([jnp.zeros((1, N_EXPERTS), I32), jnp.cumsum(count, axis=0)], axis=0).T
        off = jnp.cumsum(count, axis=1) - count
        total = cum[:, -1]
        padded = ((total + rows - 1) // rows) * rows
        pend = jnp.cumsum(padded)
        pstart = pend - padded
        n_blocks = -(-(m * TOP_K) // rows) + N_EXPERTS
        block_start = jnp.arange(n_blocks, dtype=I32) * rows
        block_e = jnp.minimum(jnp.sum((block_start[:, None] >= pend[None, :]).astype(I32), axis=1), N_EXPERTS - 1)
        n_used = (pend[-1:] // rows).astype(I32)
        block_r0 = block_start - pstart[block_e]
        block_nv = jnp.clip(total[block_e] - block_r0, 0, rows)
        cum_b = cum[block_e]
        block_jlo = jnp.sum((cum_b[:, 1:] <= block_r0[:, None]).astype(I32), axis=1)
        block_jhi = jnp.sum((cum_b[:, :-1] < (block_r0 + block_nv)[:, None]).astype(I32), axis=1)
        y = _experts(block_e, n_used, block_r0, block_nv, block_jlo, block_jhi, cum.reshape(-1), off.reshape(-1),
                     f, w_gate[l].astype(BF16), b_gate[l][:, None, :], w_up[l].astype(BF16), b_up[l][:, None, :],
                     w_down[l].astype(BF16), b_down[l][:, None, :], n_tiles, tile_rows)
        h = _combine(y, h2, pos, topg, row(final_norm_g), final=l == w_in.shape[0] - 1)
    return h.reshape(batch, seq, d)
```

```python
import functools

import numpy as np
import jax
import jax.numpy as jnp
from jax import lax
from jax.experimental import pallas as pl
from jax.experimental.pallas import tpu as pltpu

F32 = jnp.float32
BF16 = jnp.bfloat16
I32 = jnp.int32

EPS = 1e-5
ROPE_THETA = 10000.0
GLA_HEADS = 4
GLA_DK = 64
GLA_DV = 128
GLA_GATE_RANK = 16
GLA_GATE_TEMP = 16.0
GLA_CHUNK = 64
GLA_QK = GLA_HEADS * GLA_DK
GLA_V = GLA_HEADS * GLA_DV
DIL_PAIRS = ((128, 1), (512, 4), (2048, 16))
DIL_GROUPS = 3
DIL_HEADS = 4
DIL_HEAD_DIM = 128
DIL_BLOCK = 128
DIL_GW = DIL_HEADS * DIL_HEAD_DIM
DIL_W = DIL_GROUPS * DIL_GW
XATTN_HEADS = 4
XATTN_HEAD_DIM = 128
N_EXPERTS = 32
TOP_K = 4
SWIGLU_ALPHA = 1.702
SWIGLU_LIMIT = 7.0

LANES = 128
NEG_BIG = -1e30
VMEM_LIMIT = 56 * 1024 * 1024

TOKEN_TILE = 512
POST_TILE = 256
GLA_TILE = 512
MOE_ROWS = 512
MERGE_COLS = 256


def _params(*sem):
    return pltpu.CompilerParams(dimension_semantics=sem, vmem_limit_bytes=VMEM_LIMIT)


def _resident(shape):
    nd = len(shape)
    return pl.BlockSpec(shape, lambda *_: (0,) * nd, pipeline_mode=pl.Buffered(1))


def _rms(x, g):
    return x * lax.rsqrt(jnp.mean(x * x, axis=-1, keepdims=True) + EPS) * g


def _dot(a, b):
    return jnp.dot(a, b, preferred_element_type=F32)


def _dot_nt(a, b):
    return lax.dot_general(a, b, (((1,), (1,)), ((), ())), preferred_element_type=F32)


def _gla_proj_body(a, w_ref, walr_ref, wa2_ref, ba2_ref, q_ref, k_ref, v_ref, r_ref, la_ref):
    y = _dot(a, w_ref[...])
    q_ref[...] = (y[:, :GLA_QK] * (GLA_DK ** -0.5)).astype(BF16)
    k_ref[...] = y[:, GLA_QK:2 * GLA_QK].astype(BF16)
    v_ref[...] = y[:, 2 * GLA_QK:2 * GLA_QK + GLA_V].astype(BF16)
    r = y[:, 2 * GLA_QK + GLA_V:]
    r_ref[...] = (r * jax.nn.sigmoid(r)).astype(BF16)
    a_lr = _dot(a, walr_ref[...])
    logit = _dot(a_lr.astype(BF16), wa2_ref[...]) + ba2_ref[...]
    log_sig = jnp.minimum(logit, 0.0) - jnp.log1p(jnp.exp(-jnp.abs(logit)))
    la_ref[...] = log_sig * (1.0 / GLA_GATE_TEMP)


def _dil_proj_body(a, pos_ref, invf_ref, wq_ref, wk_ref, wv_ref, refs):
    q_refs, k_refs, v_refs = refs[0:3], refs[3:6], refs[6:9]
    stage = refs[9:]
    t = a.shape[0]
    ang = pos_ref[...].astype(F32) * invf_ref[...]
    lane = lax.broadcasted_iota(I32, ang.shape, 1)
    first_half = lane < DIL_HEAD_DIM // 2
    cos_sin = jnp.cos(jnp.where(first_half, ang, ang - (np.pi / 2)))
    sin_cos = pltpu.roll(cos_sin, DIL_HEAD_DIM // 2, 1)
    cos = jnp.where(first_half, cos_sin, sin_cos)
    sin_signed = jnp.where(first_half, -sin_cos, cos_sin)

    def rope(v):
        return v * cos + pltpu.roll(v, DIL_HEAD_DIM // 2, 1) * sin_signed

    post = (lambda v: rope(v) * (DIL_HEAD_DIM ** -0.5), rope, lambda v: v)
    pair = 2 * DIL_HEAD_DIM
    for ti, (w_ref, out_refs) in enumerate(((wq_ref, q_refs), (wk_ref, k_refs), (wv_ref, v_refs))):
        for g, (_, dil) in enumerate(DIL_PAIRS):
            for h2 in range(DIL_HEADS // 2):
                y = _dot(a, w_ref[:, g * DIL_GW + h2 * pair:g * DIL_GW + (h2 + 1) * pair])
                for h in (2 * h2, 2 * h2 + 1):
                    hs = slice(h * DIL_HEAD_DIM, (h + 1) * DIL_HEAD_DIM)
                    val = post[ti](y[:, (h - 2 * h2) * DIL_HEAD_DIM:(h - 2 * h2 + 1) * DIL_HEAD_DIM])
                    if dil == 1:
                        out_refs[g][0, :, hs] = val.astype(BF16)
                    else:
                        buf = stage[ti * (DIL_GROUPS - 1) + g - 1]
                        buf[h] = val
                        for p in range(dil):
                            out_refs[g][p, :, hs] = buf[h, pl.ds(p, t // dil, stride=dil), :].astype(BF16)


N_GLA_OUT = 5


def _in_proj_kernel(x_ref, pos_ref, g_ref, invf_ref, w_ref, walr_ref, wa2_ref, ba2_ref, wq_ref, wk_ref, wv_ref,
                    *refs):
    a = _rms(x_ref[...], g_ref[...]).astype(BF16)
    _gla_proj_body(a, w_ref, walr_ref, wa2_ref, ba2_ref, *refs[:N_GLA_OUT])
    _dil_proj_body(a, pos_ref, invf_ref, wq_ref, wk_ref, wv_ref, refs[N_GLA_OUT:])


def _in_proj(x2, pos2, g, invf, w, walr, wa2, ba2, wq, wk, wv, batch, seq):
    m, d = x2.shape
    t = TOKEN_TILE
    nt = seq // t
    row = lambda n: pl.BlockSpec((t, n), lambda i: (i, 0))
    out_specs = [row(GLA_QK), row(GLA_QK), row(GLA_V), row(GLA_V), row(GLA_QK)]
    out_shape = [jax.ShapeDtypeStruct((m, GLA_QK), BF16), jax.ShapeDtypeStruct((m, GLA_QK), BF16),
                 jax.ShapeDtypeStruct((m, GLA_V), BF16), jax.ShapeDtypeStruct((m, GLA_V), BF16),
                 jax.ShapeDtypeStruct((m, GLA_QK), F32)]
    for _ in range(3):
        for _, dil in DIL_PAIRS:
            out_specs.append(pl.BlockSpec((None, dil, t // dil, DIL_GW), lambda i: (i // nt, 0, i % nt, 0)))
            out_shape.append(jax.ShapeDtypeStruct((batch, dil, seq // dil, DIL_GW), BF16))
    weights = [g, invf, w, walr, wa2, ba2, wq, wk, wv]
    outs = pl.pallas_call(
        _in_proj_kernel,
        grid=(m // t,),
        in_specs=[row(d), row(1)] + [_resident(a.shape) for a in weights],
        out_specs=out_specs,
        out_shape=out_shape,
        scratch_shapes=[pltpu.VMEM((DIL_HEADS, t, DIL_HEAD_DIM), F32)] * (3 * (DIL_GROUPS - 1)),
        compiler_params=_params("parallel"),
        name="in_proj",
    )(x2, pos2, *weights)
    return outs[:N_GLA_OUT], outs[N_GLA_OUT:]


GLA_LEVELS = (32, 16, 8, 4, 2, 1)
GLA_ONE_FACTOR_MIN = -60.0


def _gla_level_masks():
    c = GLA_CHUNK
    t = np.arange(c)[:, None]
    r = np.arange(c)[None, :]
    blocks = []
    for h in GLA_LEVELS:
        u = t % (2 * h)
        m = t - u + h - 1
        blocks.append(((u >= h) & (r > m) & (r <= t)).astype(np.float32))
        blocks.append(((u < h) & (r > t) & (r <= m)).astype(np.float32))
    return np.concatenate(blocks, axis=0)


def _masked_sums(mask, x):
    x_hi = x.astype(BF16)
    x_lo = (x - x_hi.astype(F32)).astype(BF16)
    m = mask.astype(BF16)
    return _dot(m, x_hi) + _dot(m, x_lo)


def _gla_kernel(q_ref, k_ref, v_ref, r_ref, la_ref, gn_ref, tri_ref, lvl_ref, o_ref, st_ref, bc_ref):
    @pl.when(pl.program_id(1) == 0)
    def _():
        st_ref[...] = jnp.zeros_like(st_ref)

    c = GLA_CHUNK
    n_chunks = q_ref.shape[0] // c
    row_i = lax.broadcasted_iota(I32, (c, c), 0)
    col_i = lax.broadcasted_iota(I32, (c, c), 1)
    gn = gn_ref[...]

    tri = tri_ref[...]
    min_last = None
    for ci in range(n_chunks):
        rows = slice(ci * c, (ci + 1) * c)
        bcum = _masked_sums(tri, la_ref[rows, :])
        bc_ref[rows, :] = bcum
        last = bcum[c - 1:c, :]
        min_last = last if min_last is None else jnp.minimum(min_last, last)
    one_factor_ok = jnp.min(min_last) >= GLA_ONE_FACTOR_MIN

    def finish_chunk(rows, q_dec, k_dec, e_last, v, r, scores_of_head):
        for h in range(GLA_HEADS):
            ks = slice(h * GLA_DK, (h + 1) * GLA_DK)
            vs = slice(h * GLA_DV, (h + 1) * GLA_DV)
            vh = v[:, vs]
            st = st_ref[h]
            o = _dot(scores_of_head(h, ks).astype(BF16), vh) + _dot_nt(q_dec[:, ks], st.astype(BF16))
            v_t = vh.astype(F32).T.astype(BF16)
            st_ref[h] = st * e_last[:, ks] + _dot(v_t, k_dec[:, ks])
            o = o * lax.rsqrt(jnp.mean(o * o, axis=-1, keepdims=True) + EPS) * gn[:, vs]
            o_ref[rows, vs] = (o * r[:, vs]).astype(BF16)

    def load_chunk(rows):
        q = q_ref[rows, :].astype(F32)
        k = k_ref[rows, :].astype(F32)
        bcum = bc_ref[rows, :]
        blast = bcum[c - 1:c, :]
        q_dec = (q * jnp.exp(bcum)).astype(BF16)
        k_dec = (k * jnp.exp(blast - bcum)).astype(BF16)
        return q, k, bcum, q_dec, k_dec, jnp.exp(blast), v_ref[rows, :], r_ref[rows, :].astype(F32)

    @pl.when(one_factor_ok)
    def _():
        for ci in range(n_chunks):
            rows = slice(ci * c, (ci + 1) * c)
            q, k, bcum, q_dec, k_dec, e_last, v, r = load_chunk(rows)
            k_inv = (k * jnp.exp(-bcum)).astype(BF16)
            finish_chunk(rows, q_dec, k_dec, e_last, v, r,
                         lambda h, ks: jnp.where(row_i >= col_i, _dot_nt(q_dec[:, ks], k_inv[:, ks]), 0.0))

    @pl.when(jnp.logical_not(one_factor_ok))
    def _():
        def chunk(ci, carry):
            rows = pl.ds(pl.multiple_of(ci * c, c), c)
            q, k, bcum, q_dec, k_dec, e_last, v, r = load_chunk(rows)
            expo = _masked_sums(lvl_ref[...], la_ref[rows, :])
            q_lv = [(q * jnp.exp(expo[2 * j * c:(2 * j + 1) * c, :])).astype(BF16) for j in range(len(GLA_LEVELS))]
            k_lv = [(k * jnp.exp(expo[(2 * j + 1) * c:(2 * j + 2) * c, :])).astype(BF16) for j in range(len(GLA_LEVELS))]
            q_bf, k_bf = q.astype(BF16), k.astype(BF16)

            def scores_of_head(h, ks):
                sc = jnp.where(row_i == col_i, _dot_nt(q_bf[:, ks], k_bf[:, ks]), 0.0)
                differ = row_i ^ col_i
                for j, half in enumerate(GLA_LEVELS):
                    valid = (differ >= half) & (differ < 2 * half) & ((row_i & half) != 0)
                    sc = sc + jnp.where(valid, _dot_nt(q_lv[j][:, ks], k_lv[j][:, ks]), 0.0)
                return sc
            finish_chunk(rows, q_dec, k_dec, e_last, v, r, scores_of_head)
            return carry
        lax.fori_loop(0, n_chunks, chunk, 0)


def _gla(q, k, v, r, la, gn, batch, seq):
    t = GLA_TILE
    nt = seq // t
    row = lambda n: pl.BlockSpec((t, n), lambda b, j: (b * nt + j, 0))
    const = lambda a: pl.BlockSpec(a.shape, lambda b, j: (0, 0))
    tri = jnp.tril(jnp.ones((GLA_CHUNK, GLA_CHUNK), F32))
    lvl = jnp.asarray(_gla_level_masks())
    return pl.pallas_call(
        _gla_kernel,
        grid=(batch, nt),
        in_specs=[row(GLA_QK), row(GLA_QK), row(GLA_V), row(GLA_V), row(GLA_QK), const(gn), const(tri), const(lvl)],
        out_specs=row(GLA_V),
        out_shape=jax.ShapeDtypeStruct((batch * seq, GLA_V), BF16),
        scratch_shapes=[pltpu.VMEM((GLA_HEADS, GLA_DV, GLA_DK), F32), pltpu.VMEM((t, GLA_QK), F32)],
        compiler_params=_params("parallel", "arbitrary"),
        name="gla",
    )(q, k, v, r, la, gn, tri, lvl)


def _dil_attn_kernel(q_ref, k_ref, v_ref, o_ref, lse_ref, kp_ref, vp_ref):
    n = pl.program_id(2)

    @pl.when(n == 0)
    def _():
        kp_ref[...] = jnp.zeros_like(kp_ref)
        vp_ref[...] = jnp.zeros_like(vp_ref)

    qb = DIL_BLOCK
    n_blk = q_ref.shape[0] // qb
    row = lax.broadcasted_iota(I32, (qb, qb), 0)
    col = lax.broadcasted_iota(I32, (qb, qb), 1)
    lower = col <= row
    prev_bias = jnp.where(n > 0, 0.0, NEG_BIG)
    lane = lax.broadcasted_iota(I32, (qb, LANES), 1)
    blk = lambda j: slice(j * qb, (j + 1) * qb)
    lse_all = [jnp.zeros((qb, LANES), F32) for _ in range(n_blk)]
    for h in range(DIL_HEADS):
        hs = slice(h * DIL_HEAD_DIM, (h + 1) * DIL_HEAD_DIM)
        s_prev = [_dot_nt(q_ref[blk(0), hs], kp_ref[:, hs])] + [None] * (n_blk - 1)
        s_cur = [None] * n_blk
        for j in range(n_blk):
            last = min(j + 2, n_blk)
            s = _dot_nt(q_ref[j * qb:last * qb, hs], k_ref[blk(j), hs])
            s_cur[j] = s[:qb]
            if j + 1 < n_blk:
                s_prev[j + 1] = s[qb:]
        p_lower, p_upper, p_far, denom = [], [], [], []
        for j in range(n_blk):
            bias = prev_bias if j == 0 else 0.0
            s = jnp.where(lower, s_cur[j], s_prev[j] + bias)
            s_far = jnp.sum(jnp.where(row == col, s_prev[j], 0.0), axis=-1, keepdims=True) + bias
            m = jnp.maximum(jnp.max(s, axis=-1, keepdims=True), s_far)
            p = jnp.exp(s - m)
            pf = jnp.exp(s_far - m)
            l = jnp.sum(p, axis=-1, keepdims=True) + pf
            p_lower.append(jnp.where(lower, p, 0.0).astype(BF16))
            p_upper.append(jnp.where(lower, 0.0, p).astype(BF16))
            p_far.append(pf)
            denom.append(l)
            lse_all[j] = jnp.where(lane == h, m + jnp.log(l), lse_all[j])
        vp = vp_ref[:, hs]
        acc = [_dot(p_upper[0], vp) + p_far[0] * vp.astype(F32)]
        acc += [p_far[j] * v_ref[blk(j - 1), hs].astype(F32) for j in range(1, n_blk)]
        for j in range(n_blk):
            if j + 1 < n_blk:
                o2 = _dot(jnp.concatenate([p_lower[j], p_upper[j + 1]], axis=0), v_ref[blk(j), hs])
                acc[j] = acc[j] + o2[:qb]
                acc[j + 1] = acc[j + 1] + o2[qb:]
            else:
                acc[j] = acc[j] + _dot(p_lower[j], v_ref[blk(j), hs])
        for j in range(n_blk):
            o_ref[blk(j), hs] = (acc[j] / denom[j]).astype(BF16)
    for j in range(n_blk):
        lse_ref[blk(j), :] = lse_all[j]
    kp_ref[...] = k_ref[blk(n_blk - 1), :]
    vp_ref[...] = v_ref[blk(n_blk - 1), :]


DIL_BLOCKS_PER_STEP = 4


def _dil_attn(q, k, v):
    batch, dilation, length, _ = q.shape
    n_blk = max(g for g in range(1, DIL_BLOCKS_PER_STEP + 1) if (length // DIL_BLOCK) % g == 0)
    rows = n_blk * DIL_BLOCK
    blk = lambda n: pl.BlockSpec((None, None, rows, n), lambda b, p, i: (b, p, i, 0))
    return pl.pallas_call(
        _dil_attn_kernel,
        grid=(batch, dilation, length // rows),
        in_specs=[blk(DIL_GW)] * 3,
        out_specs=[blk(DIL_GW), blk(LANES)],
        out_shape=[jax.ShapeDtypeStruct(q.shape, BF16),
                   jax.ShapeDtypeStruct((batch, dilation, length, LANES), F32)],
        scratch_shapes=[pltpu.VMEM((DIL_BLOCK, DIL_GW), BF16), pltpu.VMEM((DIL_BLOCK, DIL_GW), BF16)],
        compiler_params=_params("parallel", "parallel", "arbitrary"),
        name=f"dil_attn_d{dilation}",
    )(q, k, v)


def _mem_kv_kernel(mem_ref, g_ref, w_ref, kv_ref):
    kv_ref[...] = _dot(_rms(mem_ref[...], g_ref[...]).astype(BF16), w_ref[...]).astype(BF16)


def _mem_kv(mem, g, w):
    b, ml, d = mem.shape
    n = w.shape[1]
    return pl.pallas_call(
        _mem_kv_kernel,
        grid=(b,),
        in_specs=[pl.BlockSpec((None, ml, d), lambda i: (i, 0, 0)), _resident(g.shape), _resident(w.shape)],
        out_specs=pl.BlockSpec((None, ml, n), lambda i: (i, 0, 0)),
        out_shape=jax.ShapeDtypeStruct((b, ml, n), BF16),
        compiler_params=_params("parallel"),
        name="mem_kv",
    )(mem, g, w)


def _post_mixer_kernel(x_ref, og_ref, od0_ref, od1_ref, od2_ref, l0_ref, l1_ref, l2_ref, kv_ref,
                       g1_ref, wgate_ref, bgate_ref, wbg_ref, wbd_ref, wmix_ref,
                       g2_ref, wxq_ref, wxo_ref, g3_ref, wrh_ref, wrl_ref, br_ref,
                       h_ref, f_ref, pos_ref, topg_ref, cnt_ref,
                       o1_buf, o2_buf, l1_buf, l2_buf):
    x = x_ref[...]
    t, d = x.shape
    a = _rms(x, g1_ref[...]).astype(BF16)

    for od_ref, l_ref, o_buf, l_buf in ((od1_ref, l1_ref, o1_buf, l1_buf), (od2_ref, l2_ref, o2_buf, l2_buf)):
        dil = od_ref.shape[0]
        for p in range(dil):
            rows = pl.ds(p, t // dil, stride=dil)
            l_buf[rows, :] = l_ref[p]
            for h in range(DIL_HEADS):
                o_buf[h, rows, :] = od_ref[p, :, h * DIL_HEAD_DIM:(h + 1) * DIL_HEAD_DIM].astype(F32)

    l0, l1, l2 = l0_ref[0], l1_buf[...], l2_buf[...]
    lm = jnp.maximum(jnp.maximum(l0, l1), l2)
    e0, e1, e2 = jnp.exp(l0 - lm), jnp.exp(l1 - lm), jnp.exp(l2 - lm)
    inv = 1.0 / (e0 + e1 + e2)
    w0, w1, w2 = e0 * inv, e1 * inv, e2 * inv
    od_heads = []
    for h in range(DIL_HEADS):
        hs = slice(h * DIL_HEAD_DIM, (h + 1) * DIL_HEAD_DIM)
        od_heads.append(w0[:, h:h + 1] * od0_ref[0, :, hs].astype(F32)
                        + w1[:, h:h + 1] * o1_buf[h] + w2[:, h:h + 1] * o2_buf[h])
    o_d = jnp.concatenate(od_heads, axis=-1).astype(BF16)

    og = og_ref[...]
    merged = []
    for c0 in range(0, d, MERGE_COLS):
        cs, cs_dil = slice(c0, c0 + MERGE_COLS), slice(d + c0, d + c0 + MERGE_COLS)
        gate_gla = jax.nn.sigmoid(_dot(a, wgate_ref[:, cs]) + bgate_ref[:, cs])
        gate_dil = jax.nn.sigmoid(_dot(a, wgate_ref[:, cs_dil]) + bgate_ref[:, cs_dil])
        merged.append((gate_gla * _dot(og, wbg_ref[:, cs]) + gate_dil * _dot(o_d, wbd_ref[:, cs])).astype(BF16))
    h1 = x + _dot(jnp.concatenate(merged, axis=-1), wmix_ref[...])

    c = _rms(h1, g2_ref[...]).astype(BF16)
    qx = (_dot(c, wxq_ref[...]) * (XATTN_HEAD_DIM ** -0.5)).astype(BF16)
    xw = XATTN_HEADS * XATTN_HEAD_DIM
    heads = []
    for h in range(XATTN_HEADS):
        hs = slice(h * XATTN_HEAD_DIM, (h + 1) * XATTN_HEAD_DIM)
        vs = slice(xw + h * XATTN_HEAD_DIM, xw + (h + 1) * XATTN_HEAD_DIM)
        s = _dot_nt(qx[:, hs], kv_ref[:, hs])
        p = jnp.exp(s - jnp.max(s, axis=-1, keepdims=True))
        o = _dot(p.astype(BF16), kv_ref[:, vs]) / jnp.sum(p, axis=-1, keepdims=True)
        heads.append(o.astype(BF16))
    h2 = h1 + _dot(jnp.concatenate(heads, axis=-1), wxo_ref[...])
    h_ref[...] = h2

    f = _rms(h2, g3_ref[...])
    f_hi = f.astype(BF16)
    f_lo = (f - f_hi.astype(F32)).astype(BF16)
    logits = (_dot(f_hi, wrh_ref[...]) + (_dot(f_lo, wrh_ref[...]) + _dot(f_hi, wrl_ref[...]))) + br_ref[...]
    lane = lax.broadcasted_iota(I32, (t, LANES), 1)
    work = jnp.where(lane < N_EXPERTS, logits, NEG_BIG)
    selected = jnp.zeros((t, LANES), F32)
    top_val, top_idx = [], []
    lane_f = lane.astype(F32)
    for _ in range(TOP_K):
        mx = jnp.max(work, axis=-1, keepdims=True)
        idx = jnp.min(jnp.where(work == mx, lane_f, float(LANES)), axis=-1, keepdims=True)
        hit = lane_f == idx
        selected = jnp.where(hit, 1.0, selected)
        work = jnp.where(hit, NEG_BIG, work)
        top_val.append(mx)
        top_idx.append(idx)
    ex = [jnp.exp(v - top_val[0]) for v in top_val]
    inv_den = 1.0 / (ex[0] + ex[1] + ex[2] + ex[3])

    sel16 = selected.astype(BF16)
    earlier = lax.broadcasted_iota(I32, (t, t), 1) < lax.broadcasted_iota(I32, (t, t), 0)
    before = _dot(jnp.where(earlier, 1.0, 0.0).astype(BF16), sel16)
    count = jnp.sum(selected, axis=0, keepdims=True)
    lower_expert = lax.broadcasted_iota(I32, (LANES, LANES), 0) < lax.broadcasted_iota(I32, (LANES, LANES), 1)
    offset = _dot(jnp.broadcast_to(count, (8, LANES)).astype(BF16),
                  jnp.where(lower_expert, 1.0, 0.0).astype(BF16))[0:1, :]
    where_to = before + offset
    pos = jnp.zeros((t, LANES), F32)
    topg = jnp.zeros((t, LANES), F32)
    for kk in range(TOP_K):
        pk = jnp.sum(jnp.where(lane_f == top_idx[kk], where_to, 0.0), axis=-1, keepdims=True)
        pos = jnp.where(lane == kk, pk, pos)
        topg = jnp.where(lane == kk, ex[kk] * inv_den, topg)
    pos_ref[...] = pos.astype(I32)
    topg_ref[...] = topg
    cnt_ref[...] = jnp.broadcast_to(count, cnt_ref.shape)

    pos_rows = pos.T
    slot_i = lax.broadcasted_iota(I32, (t * TOP_K, t), 0).astype(F32)
    perm = jnp.zeros((t * TOP_K, t), F32)
    for kk in range(TOP_K):
        perm = jnp.where(slot_i == pos_rows[kk:kk + 1, :], 1.0, perm)
    f_sorted = _dot(perm.astype(BF16), f_hi)
    for j in range(d // LANES):
        f_ref[pl.ds(j, t * TOP_K, stride=d // LANES), :] = f_sorted[:, j * LANES:(j + 1) * LANES]


def _post_mixer(x2, og, ods, lses, kv, seq, g1, wgate, bgate, wbg, wbd, wmix, g2, wxq, wxo, g3, wr, br):
    wr_hi = wr.astype(BF16)
    wr_lo = (wr - wr_hi.astype(F32)).astype(BF16)
    m, d = x2.shape
    t = POST_TILE
    nt = seq // t
    sub = d // LANES
    row = lambda n: pl.BlockSpec((t, n), lambda i: (i, 0))
    phase = lambda a: pl.BlockSpec((None, a.shape[1], t // a.shape[1], a.shape[3]),
                                   lambda i: (i // nt, 0, i % nt, 0))
    kv_spec = pl.BlockSpec((None,) + kv.shape[1:], lambda i: (i // nt, 0, 0))
    weights = [g1, wgate, bgate, wbg, wbd, wmix, g2, wxq, wxo, g3, wr_hi, wr_lo, br]
    return pl.pallas_call(
        _post_mixer_kernel,
        grid=(m // t,),
        in_specs=[row(d), row(GLA_V)] + [phase(a) for a in ods] + [phase(a) for a in lses] + [kv_spec]
                 + [_resident(w.shape) for w in weights],
        out_specs=[row(d), pl.BlockSpec((t * TOP_K * sub, LANES), lambda i: (i, 0)), row(LANES), row(LANES),
                   pl.BlockSpec((8, LANES), lambda i: (i, 0))],
        out_shape=[jax.ShapeDtypeStruct((m, d), F32), jax.ShapeDtypeStruct((m * TOP_K * sub, LANES), F32),
                   jax.ShapeDtypeStruct((m, LANES), I32), jax.ShapeDtypeStruct((m, LANES), F32),
                   jax.ShapeDtypeStruct((m // t * 8, LANES), F32)],
        scratch_shapes=[pltpu.VMEM((DIL_HEADS, t, DIL_HEAD_DIM), F32), pltpu.VMEM((DIL_HEADS, t, DIL_HEAD_DIM), F32),
                        pltpu.VMEM((t, LANES), F32), pltpu.VMEM((t, LANES), F32)],
        compiler_params=_params("parallel"),
        name="post_mixer",
    )(x2, og, *ods, *lses, kv, *weights)


ROW_SUB = 8
RUN_BITS = tuple(1 << b for b in range(MOE_ROWS.bit_length() - 1, -1, -1))
RUN_SMALL = 64


def _wait_tiles(hbm, vmem, sem):
    pltpu.make_async_copy(hbm.at[pl.ds(0, vmem.shape[0]), :], vmem, sem).wait()


def _experts_kernel(be_ref, nu_ref, r0_ref, nv_ref, jlo_ref, jhi_ref, cum_ref, off_ref,
                    f_hbm, wg_ref, bg_ref, wu_ref, bu_ref, wd_ref, bd_ref, y_hbm,
                    xbuf, ybuf, in_sems, out_sems, *, n_tiles, tile_rows):
    i = pl.program_id(0)
    n_used = nu_ref[0]
    rows = xbuf.shape[1] // ROW_SUB
    spare_row = n_tiles * tile_rows

    def copy_rows(hbm_row, buf_row, count, start_piece):
        def piece(bit):
            @pl.when((count & bit) != 0)
            def _():
                taken = count & (-2 * bit)
                start_piece(hbm_row + taken, buf_row + taken, bit)

        @pl.when(count >= RUN_SMALL)
        def _():
            for bit in RUN_BITS:
                if RUN_SMALL <= bit <= rows:
                    piece(bit)
        for bit in RUN_BITS:
            if bit < RUN_SMALL:
                piece(bit)

    def for_runs(b, start_piece, filler_row):
        e, r0, nv = be_ref[b], r0_ref[b], nv_ref[b]

        def body(j, carry):
            lo, hi = cum_ref[e * (n_tiles + 1) + j], cum_ref[e * (n_tiles + 1) + j + 1]
            first = jnp.maximum(lo, r0)
            count = jnp.maximum(jnp.minimum(hi, r0 + nv) - first, 0)
            copy_rows(j * tile_rows + off_ref[j * N_EXPERTS + e] + first - lo, first - r0, count, start_piece)
            return carry
        lax.fori_loop(jlo_ref[b], jhi_ref[b], body, 0)

        @pl.when(nv < rows)
        def _():
            copy_rows(filler_row, nv, rows - nv, start_piece)

    def gather(b, s):
        def piece(hbm_row, buf_row, n):
            pltpu.make_async_copy(f_hbm.at[pl.ds(hbm_row * ROW_SUB, n * ROW_SUB), :],
                                  xbuf.at[s, pl.ds(buf_row * ROW_SUB, n * ROW_SUB), :], in_sems.at[s]).start()
        for_runs(b, piece, 0)

    def scatter(b, s):
        def piece(hbm_row, buf_row, n):
            pltpu.make_async_copy(ybuf.at[s, pl.ds(buf_row * ROW_SUB, n * ROW_SUB), :],
                                  y_hbm.at[pl.ds(hbm_row * ROW_SUB, n * ROW_SUB), :], out_sems.at[s]).start()
        for_runs(b, piece, spare_row + s * rows)

    xs, ys = lax.rem(i, 2), lax.rem(i, 3)

    @pl.when(i == 0)
    def _():
        ybuf[2] = jnp.zeros(ybuf.shape[1:], F32)
        spare = [pltpu.make_async_copy(ybuf.at[2], y_hbm.at[pl.ds((spare_row + s * rows) * ROW_SUB, rows * ROW_SUB), :],
                                       out_sems.at[2]) for s in range(3)]
        for cp in spare:
            cp.start()
        for cp in spare:
            cp.wait()

    @pl.when((i >= 3) & (i - 3 < n_used))
    def _():
        _wait_tiles(y_hbm, ybuf.at[ys], out_sems.at[ys])

    @pl.when((i == 0) & (n_used > 0))
    def _():
        gather(0, 0)

    @pl.when(i + 1 < n_used)
    def _():
        gather(i + 1, lax.rem(i + 1, 2))

    @pl.when((i >= 1) & (i - 1 < n_used))
    def _():
        scatter(i - 1, lax.rem(i + 2, 3))

    @pl.when(i < n_used)
    def _():
        _wait_tiles(f_hbm, xbuf.at[xs], in_sems.at[xs])
        xb = jnp.concatenate([xbuf[xs, pl.ds(j, rows, stride=ROW_SUB), :] for j in range(ROW_SUB)],
                             axis=-1).astype(BF16)
        glu = jnp.minimum(_dot(xb, wg_ref[...]) + bg_ref[...], SWIGLU_LIMIT)
        lin = jnp.clip(_dot(xb, wu_ref[...]) + bu_ref[...], -SWIGLU_LIMIT, SWIGLU_LIMIT)
        act = glu * jax.nn.sigmoid(SWIGLU_ALPHA * glu) * (lin + 1.0)
        y = _dot(act.astype(BF16), wd_ref[...]) + bd_ref[...]
        for j in range(ROW_SUB):
            ybuf[ys, pl.ds(j, rows, stride=ROW_SUB), :] = y[:, j * LANES:(j + 1) * LANES]


def _experts(block_e, n_used, block_r0, block_nv, block_jlo, block_jhi, cum_flat, off_flat,
             f, wg, bg, wu, bu, wd, bd, n_tiles, tile_rows):
    rows = MOE_ROWS
    n_blocks = block_e.shape[0]
    wspec = lambda w: pl.BlockSpec((None,) + w.shape[1:],
                                   lambda i, be, *_: (be[jnp.minimum(i, n_blocks - 1)], 0, 0))
    return pl.pallas_call(
        functools.partial(_experts_kernel, n_tiles=n_tiles, tile_rows=tile_rows),
        grid_spec=pltpu.PrefetchScalarGridSpec(
            num_scalar_prefetch=8,
            grid=(n_blocks + 3,),
            in_specs=[pl.BlockSpec(memory_space=pl.ANY),
                      wspec(wg), wspec(bg), wspec(wu), wspec(bu), wspec(wd), wspec(bd)],
            out_specs=pl.BlockSpec(memory_space=pl.ANY),
            scratch_shapes=[pltpu.VMEM((2, rows * ROW_SUB, LANES), F32), pltpu.VMEM((3, rows * ROW_SUB, LANES), F32),
                            pltpu.SemaphoreType.DMA((2,)), pltpu.SemaphoreType.DMA((3,))],
        ),
        out_shape=jax.ShapeDtypeStruct(((n_tiles * tile_rows + 3 * rows) * ROW_SUB, LANES), F32),
        compiler_params=_params("arbitrary"),
        name="experts",
    )(block_e, n_used, block_r0, block_nv, block_jlo, block_jhi, cum_flat, off_flat, f, wg, bg, wu, bu, wd, bd)


def _combine_kernel(y_ref, h_ref, pos_ref, topg_ref, g_ref, o_ref, *, final):
    t, d = h_ref.shape
    n_sorted = t * TOP_K
    pos = pos_ref[...]
    gate = topg_ref[...]
    slot_i = lax.broadcasted_iota(I32, (t, n_sorted), 1)
    w = jnp.zeros((t, n_sorted), F32)
    for kk in range(TOP_K):
        w = jnp.where(slot_i == pos[:, kk:kk + 1], gate[:, kk:kk + 1], w)
    y = jnp.concatenate([y_ref[pl.ds(j, n_sorted, stride=ROW_SUB), :] for j in range(ROW_SUB)], axis=-1)
    acc = h_ref[...] + _dot(w.astype(BF16), y.astype(BF16))
    o_ref[...] = _rms(acc, g_ref[...]) if final else acc


def _combine(y, h, pos, topg, g, final):
    m, d = h.shape
    t = POST_TILE
    row = lambda n: pl.BlockSpec((t, n), lambda i: (i, 0))
    return pl.pallas_call(
        functools.partial(_combine_kernel, final=final),
        grid=(m // t,),
        in_specs=[pl.BlockSpec((t * TOP_K * ROW_SUB, LANES), lambda i: (i, 0)), row(d), row(LANES), row(LANES),
                  pl.BlockSpec(g.shape, lambda i: (0, 0))],
        out_specs=row(d),
        out_shape=jax.ShapeDtypeStruct((m, d), F32),
        compiler_params=_params("parallel"),
        name="combine",
    )(y, h, pos, topg, g)


def _pad_lanes(w, axis):
    pad = [(0, 0)] * w.ndim
    pad[axis] = (0, LANES - w.shape[axis])
    return jnp.pad(w, pad)


def kernel(x, mem, positions, norm1_g, w_in, w_gla_a2, b_gla_a2, gla_norm_g, w_branch_gla, w_branch_dil, w_branch_gate, b_branch_gate, w_mix_out, norm2_g, mem_norm_g, w_xq, w_xkv, w_xo, norm3_g, w_router, b_router, w_gate, b_gate, w_up, b_up, w_down, b_down, final_norm_g):
    batch, seq, d = x.shape
    m = batch * seq
    h = x.reshape(m, d)
    pos2 = positions.reshape(m, 1)
    row = lambda v: v.reshape(1, -1)
    half = DIL_HEAD_DIM // 2
    inv_freq = ROPE_THETA ** (-jnp.arange(half, dtype=F32) / half)
    invf = jnp.concatenate([inv_freq, inv_freq]).reshape(1, DIL_HEAD_DIM)
    off_alr = 2 * GLA_QK + 2 * GLA_V
    off_q = off_alr + GLA_GATE_RANK

    for l in range(w_in.shape[0]):
        w_l = w_in[l]
        g1 = row(norm1_g[l])
        (q_g, k_g, v_g, r_g, log_a), dil = _in_proj(
            h, pos2, g1, invf, w_l[:, :off_alr].astype(BF16), _pad_lanes(w_l[:, off_alr:off_q], 1).astype(BF16),
            _pad_lanes(w_gla_a2[l], 0).astype(BF16), row(b_gla_a2[l]),
            w_l[:, off_q:off_q + DIL_W].astype(BF16), w_l[:, off_q + DIL_W:off_q + 2 * DIL_W].astype(BF16),
            w_l[:, off_q + 2 * DIL_W:].astype(BF16), batch, seq)
        o_g = _gla(q_g, k_g, v_g, r_g, log_a, row(gla_norm_g[l]), batch, seq)
        ods, lses = [], []
        for g in range(DIL_GROUPS):
            o_i, lse_i = _dil_attn(dil[g], dil[DIL_GROUPS + g], dil[2 * DIL_GROUPS + g])
            ods.append(o_i)
            lses.append(lse_i)
        kv = _mem_kv(mem, row(mem_norm_g[l]), w_xkv[l].astype(BF16))
        h2, f, pos, topg, cnt = _post_mixer(
            h, o_g, ods, lses, kv, seq, g1, w_branch_gate[l].astype(BF16), row(b_branch_gate[l]),
            w_branch_gla[l].astype(BF16), w_branch_dil[l].astype(BF16), w_mix_out[l].astype(BF16),
            row(norm2_g[l]), w_xq[l].astype(BF16), w_xo[l].astype(BF16), row(norm3_g[l]),
            _pad_lanes(w_router[l], 1), _pad_lanes(row(b_router[l]), 1))

        rows = MOE_ROWS
        n_tiles = m // POST_TILE
        tile_rows = POST_TILE * TOP_K
        count = cnt.reshape(n_tiles, 8, LANES)[:, 0, :N_EXPERTS].astype(I32)
        cum = jnp.concatenate([jnp.zeros((1, N_EXPERTS), I32), jnp.cumsum(count, axis=0)], axis=0).T
        off = jnp.cumsum(count, axis=1) - count
        total = cum[:, -1]
        padded = ((total + rows - 1) // rows) * rows
        pend = jnp.cumsum(padded)
        pstart = pend - padded
        n_blocks = -(-(m * TOP_K) // rows) + N_EXPERTS
        block_start = jnp.arange(n_blocks, dtype=I32) * rows
        block_e = jnp.minimum(jnp.sum((block_start[:, None] >= pend[None, :]).astype(I32), axis=1), N_EXPERTS - 1)
        n_used = (pend[-1:] // rows).astype(I32)
        block_r0 = block_start - pstart[block_e]
        block_nv = jnp.clip(total[block_e] - block_r0, 0, rows)
        cum_b = cum[block_e]
        block_jlo = jnp.sum((cum_b[:, 1:] <= block_r0[:, None]).astype(I32), axis=1)
        block_jhi = jnp.sum((cum_b[:, :-1] < (block_r0 + block_nv)[:, None]).astype(I32), axis=1)
        y = _experts(block_e, n_used, block_r0, block_nv, block_jlo, block_jhi, cum.reshape(-1), off.reshape(-1),
                     f, w_gate[l].astype(BF16), b_gate[l][:, None, :], w_up[l].astype(BF16), b_up[l][:, None, :],
                     w_down[l].astype(BF16), b_down[l][:, None, :], n_tiles, tile_rows)
        h = _combine(y, h2, pos, topg, row(final_norm_g), final=l == w_in.shape[0] - 1)
    return h.reshape(batch, seq, d)
```
